```python
import jax, jax.numpy as jnp
from jax import lax
import numpy as np

D_MODEL = 2048
BATCH = 32
SEQ = 256
DEPTH = 4
DEC_BATCH = 2
DEC_SEQ = 2048
PAST_LEN = 512

GRID_W = 64
HEAD_DIM = 64
ATTN_W = D_MODEL // 2
N_HEADS = ATTN_W // HEAD_DIM
N_KV_HEADS = 4
GQA_GROUP = N_HEADS // N_KV_HEADS
KV_W = N_KV_HEADS * HEAD_DIM
POOL_W = D_MODEL // 4
POOL_WINDOWS = (2, 4, 8, 16)
POOL_GROUPS = len(POOL_WINDOWS)
POOL_GW = POOL_W // POOL_GROUPS
CONV_W = D_MODEL // 4
CONV_K = 31
MIX_W = ATTN_W + POOL_W + CONV_W
IN_W = ATTN_W + 2 * KV_W + POOL_W + 2 * CONV_W
WINDOW = 128
BLOCK = 128
D_FF = 4 * D_MODEL
ROPE_BASE = 10000.0
EPS = 1e-6
NEG = -1e30

kernel_name = 'hybrid_pool_swa_conformer_flow_step'


def rms_norm(x, g):
    xf = x.astype(jnp.float32)
    y = xf * lax.rsqrt(jnp.mean(xf * xf, axis=-1, keepdims=True) + EPS)
    return (y * g.astype(jnp.float32)).astype(x.dtype)


def layer_norm(x, g, b):
    xf = x.astype(jnp.float32)
    mu = jnp.mean(xf, axis=-1, keepdims=True)
    var = jnp.mean(jnp.square(xf - mu), axis=-1, keepdims=True)
    y = (xf - mu) * lax.rsqrt(var + EPS)
    return (y * g.astype(jnp.float32) + b.astype(jnp.float32)).astype(x.dtype)


def rope_2d(x, rows):
    half = HEAD_DIM // 2
    quarter = half // 2
    inv_freq = 1.0 / (ROPE_BASE ** (jnp.arange(quarter, dtype=jnp.float32) / quarter))
    row_pos = jnp.repeat(jnp.arange(rows, dtype=jnp.float32), GRID_W)
    col_pos = jnp.tile(jnp.arange(GRID_W, dtype=jnp.float32), rows)

    def rotate(u, pos):
        ang = pos[:, None] * inv_freq[None, :]
        cos = jnp.cos(ang)[None, :, None, :]
        sin = jnp.sin(ang)[None, :, None, :]
        u1, u2 = u[..., :quarter], u[..., quarter:]
        return jnp.concatenate([u1 * cos - u2 * sin, u2 * cos + u1 * sin], axis=-1)

    xf = x.astype(jnp.float32)
    out = jnp.concatenate([rotate(xf[..., :half], row_pos), rotate(xf[..., half:], col_pos)], axis=-1)
    return out.astype(x.dtype)


def sink_softmax(s, sink_b):
    sk = jnp.broadcast_to(sink_b.astype(jnp.float32), s.shape[:-1] + (1,))
    return jax.nn.softmax(jnp.concatenate([s, sk], axis=-1), axis=-1)[..., :-1]


def ctx_attention(q, k, v, sink):
    B, S = q.shape[0], q.shape[1]
    qg = q.reshape(B, S, N_KV_HEADS, GQA_GROUP, HEAD_DIM)
    s = jnp.einsum('bqkgd,bpkd->bkgqp', qg, k).astype(jnp.float32) * (HEAD_DIM ** -0.5)
    p = sink_softmax(s, sink.reshape(1, N_KV_HEADS, GQA_GROUP, 1, 1))
    o = jnp.einsum('bkgqp,bpkd->bqkgd', p.astype(v.dtype), v)
    return o.reshape(B, S, ATTN_W)


def latent_attention(q, k, v, ck, cv, sink):
    B, T = q.shape[0], q.shape[1]
    nb = T // BLOCK
    qb = q.reshape(B, nb, BLOCK, N_KV_HEADS, GQA_GROUP, HEAD_DIM)
    pad = ((0, 0), (BLOCK, BLOCK), (0, 0), (0, 0))
    kp = jnp.pad(k, pad).reshape(B, nb + 2, BLOCK, N_KV_HEADS, HEAD_DIM)
    vp = jnp.pad(v, pad).reshape(B, nb + 2, BLOCK, N_KV_HEADS, HEAD_DIM)
    kw = jnp.concatenate([kp[:, :-2], kp[:, 1:-1], kp[:, 2:]], axis=2)
    vw = jnp.concatenate([vp[:, :-2], vp[:, 1:-1], vp[:, 2:]], axis=2)
    qi = jnp.arange(BLOCK)
    kj = jnp.arange(3 * BLOCK)
    bi = jnp.arange(nb)
    rel = kj[None, :] - BLOCK - qi[:, None]
    kpos = bi[:, None] * BLOCK - BLOCK + kj[None, :]
    valid = (jnp.abs(rel) <= WINDOW)[None] & ((kpos >= 0) & (kpos < T))[:, None, :]
    scale = HEAD_DIM ** -0.5
    s_loc = jnp.einsum('bnqkgd,bnskd->bnkgqs', qb, kw).astype(jnp.float32) * scale
    s_loc = jnp.where(valid[None, :, None, None], s_loc, NEG)
    s_ctx = jnp.einsum('bnqkgd,bpkd->bnkgqp', qb, ck).astype(jnp.float32) * scale
    n_loc = 3 * BLOCK
    p = sink_softmax(jnp.concatenate([s_loc, s_ctx], axis=-1),
                     sink.reshape(1, 1, N_KV_HEADS, GQA_GROUP, 1, 1)).astype(v.dtype)
    o = (jnp.einsum('bnkgqs,bnskd->bnqkgd', p[..., :n_loc], vw)
         + jnp.einsum('bnkgqp,bpkd->bnqkgd', p[..., n_loc:], cv))
    return o.reshape(B, T, ATTN_W)


def pool_mixer(u, w_pool, pool_scale):
    B, S = u.shape[0], u.shape[1]
    uf = u.astype(jnp.float32).reshape(B, S, POOL_GROUPS, POOL_GW)
    cs = jnp.concatenate([jnp.zeros((B, 1, POOL_GROUPS, POOL_GW), jnp.float32),
                          lax.cumsum(uf, axis=1)], axis=1)
    t = jnp.arange(S)
    halfw = jnp.array([w // 2 for w in POOL_WINDOWS])
    wins = jnp.array(POOL_WINDOWS)
    lo = jnp.clip(t[:, None] - halfw[None, :], 0, S)
    hi = jnp.clip(t[:, None] - halfw[None, :] + wins[None, :], 0, S)
    gidx = jnp.arange(POOL_GROUPS)[None, :]
    wsum = cs[:, hi, gidx] - cs[:, lo, gidx]
    mean = wsum / (hi - lo).astype(jnp.float32)[None, :, :, None]
    pooled = (mean - uf).astype(u.dtype)
    y = jnp.einsum('bsgc,gcd->bsgd', pooled, w_pool).reshape(B, S, POOL_W)
    return y * pool_scale


def conv_mixer(u, conv_dw, conv_b, ln_g, ln_b, w_pw):
    a, g = jnp.split(u, 2, axis=-1)
    h = a * jax.nn.sigmoid(g)
    h = lax.conv_general_dilated(h, conv_dw[:, None, :], window_strides=(1,),
                                 padding=[(CONV_K // 2, CONV_K // 2)],
                                 dimension_numbers=('NWC', 'WIO', 'NWC'),
                                 feature_group_count=CONV_W) + conv_b
    h = jax.nn.silu(layer_norm(h, ln_g, ln_b))
    return h @ w_pw


def trunk_layer(x, cvec, ctx_kv, lp):
    mods = jnp.split(jax.nn.silu(cvec) @ lp['w_ada'] + lp['b_ada'], 6, axis=-1)
    sh1, sc1, g1, sh2, sc2, g2 = [m[:, None, :] for m in mods]
    B, S = x.shape[0], x.shape[1]
    h = rms_norm(x, lp['g_pre1']) * (1.0 + sc1) + sh1
    z = h @ lp['w_in']
    o1 = ATTN_W
    o2 = o1 + KV_W
    o3 = o2 + KV_W
    o4 = o3 + POOL_W
    q = z[..., :o1].reshape(B, S, N_HEADS, HEAD_DIM)
    k = z[..., o1:o2].reshape(B, S, N_KV_HEADS, HEAD_DIM)
    v = z[..., o2:o3].reshape(B, S, N_KV_HEADS, HEAD_DIM)
    pu = z[..., o3:o4]
    cu = z[..., o4:]
    if ctx_kv is None:
        attn = ctx_attention(q, k, v, lp['sink'])
        kv_out = (k, v)
    else:
        rows = S // GRID_W
        attn = latent_attention(rope_2d(q, rows), rope_2d(k, rows), v, ctx_kv[0], ctx_kv[1], lp['sink'])
        kv_out = None
    pool = pool_mixer(pu, lp['w_pool'], lp['pool_scale'])
    conv = conv_mixer(cu, lp['conv_dw'], lp['conv_b'], lp['conv_ln_g'], lp['conv_ln_b'], lp['w_conv_pw'])
    mix = jnp.concatenate([attn, pool, conv], axis=-1) @ lp['w_out']
    x = x + g1 * rms_norm(mix, lp['g_post1'])
    h = rms_norm(x, lp['g_pre2']) * (1.0 + sc2) + sh2
    f = jnp.square(jax.nn.relu(h @ lp['w_mlp1'])) @ lp['w_mlp2']
    x = x + g2 * rms_norm(f, lp['g_post2'])
    return x, kv_out


def setup_inputs(seed: int = 0) -> dict:
    key = jax.random.key(seed)
    ks = jax.random.split(key, 24)

    def nrm(k, shape, scale):
        return jax.random.normal(k, shape, jnp.float32) * scale

    return {
        'x_prompt': nrm(ks[0], (BATCH, SEQ, D_MODEL), 1.0),
        'x_sample': nrm(ks[1], (DEC_BATCH, DEC_SEQ, D_MODEL), 1.0),
        'cache_k': nrm(ks[2], (DEC_BATCH, DEPTH, PAST_LEN, N_KV_HEADS, HEAD_DIM), 1.0),
        'cache_v': nrm(ks[3], (DEC_BATCH, DEPTH, PAST_LEN, N_KV_HEADS, HEAD_DIM), 1.0),
        'c': nrm(ks[4], (DEC_BATCH, D_MODEL), 1.0),
        'c_ctx': nrm(ks[5], (D_MODEL,), 1.0),
        'w_ada': nrm(ks[6], (DEPTH, D_MODEL, 6 * D_MODEL), D_MODEL ** -0.5),
        'b_ada': nrm(ks[7], (DEPTH, 6 * D_MODEL), 0.02),
        'g_pre1': 1.0 + nrm(ks[8], (DEPTH, D_MODEL), 0.02),
        'g_post1': 1.0 + nrm(ks[9], (DEPTH, D_MODEL), 0.02),
        'g_pre2': 1.0 + nrm(ks[10], (DEPTH, D_MODEL), 0.02),
        'g_post2': 1.0 + nrm(ks[11], (DEPTH, D_MODEL), 0.02),
        'w_in': nrm(ks[12], (DEPTH, D_MODEL, IN_W), D_MODEL ** -0.5),
        'sink': nrm(ks[13], (DEPTH, N_HEADS), 0.5),
        'w_pool': nrm(ks[14], (DEPTH, POOL_GROUPS, POOL_GW, POOL_GW), POOL_GW ** -0.5),
        'pool_scale': 1.0 + nrm(ks[15], (DEPTH, POOL_W), 0.1),
        'conv_dw': nrm(ks[16], (DEPTH, CONV_K, CONV_W), CONV_K ** -0.5),
        'conv_b': nrm(ks[17], (DEPTH, CONV_W), 0.02),
        'conv_ln_g': 1.0 + nrm(ks[18], (DEPTH, CONV_W), 0.02),
        'conv_ln_b': nrm(ks[19], (DEPTH, CONV_W), 0.02),
        'w_conv_pw': nrm(ks[20], (DEPTH, CONV_W, CONV_W), CONV_W ** -0.5),
        'w_out': nrm(ks[21], (DEPTH, MIX_W, D_MODEL), MIX_W ** -0.5),
        'w_mlp1': nrm(ks[22], (DEPTH, D_MODEL, D_FF), D_MODEL ** -0.5),
        'w_mlp2': nrm(ks[23], (DEPTH, D_FF, D_MODEL), D_FF ** -0.5),
    }


def reference(x_prompt, x_sample, cache_k, cache_v, c, c_ctx, w_ada, b_ada, g_pre1, g_post1,
              g_pre2, g_post2, w_in, sink, w_pool, pool_scale, conv_dw, conv_b, conv_ln_g,
              conv_ln_b, w_conv_pw, w_out, w_mlp1, w_mlp2):
    xp = x_prompt
    xs = x_sample
    cvec_ctx = c_ctx[None, :]
    new_ks = []
    new_vs = []
    for l in range(DEPTH):
        lp = {
            'w_ada': w_ada[l], 'b_ada': b_ada[l],
            'g_pre1': g_pre1[l], 'g_post1': g_post1[l], 'g_pre2': g_pre2[l], 'g_post2': g_post2[l],
            'w_in': w_in[l], 'sink': sink[l], 'w_pool': w_pool[l], 'pool_scale': pool_scale[l],
            'conv_dw': conv_dw[l], 'conv_b': conv_b[l], 'conv_ln_g': conv_ln_g[l],
            'conv_ln_b': conv_ln_b[l], 'w_conv_pw': w_conv_pw[l], 'w_out': w_out[l],
            'w_mlp1': w_mlp1[l], 'w_mlp2': w_mlp2[l],
        }
        xp, (k_l, v_l) = trunk_layer(xp, cvec_ctx, None, lp)
        new_ks.append(k_l)
        new_vs.append(v_l)
        xs, _ = trunk_layer(xs, c, (cache_k[:, l], cache_v[:, l]), lp)
    new_k = jnp.stack(new_ks, axis=1)
    new_v = jnp.stack(new_vs, axis=1)
    return (xp, xs, new_k, new_v)
```

```python
import functools

import numpy as np
import jax
import jax.numpy as jnp
from jax import lax
from jax.experimental import pallas as pl
from jax.experimental.pallas import tpu as pltpu

D_MODEL = 2048
BATCH = 32
SEQ = 256
DEPTH = 4
DEC_BATCH = 2
DEC_SEQ = 2048
PAST_LEN = 512
GRID_W = 64
HEAD_DIM = 64
ATTN_W = 1024
N_HEADS = 16
N_KV_HEADS = 4
KV_W = 256
POOL_W = 512
POOL_WINDOWS = (2, 4, 8, 16)
POOL_GW = 128
CONV_W = 512
CONV_K = 31
IN_W = 3072
WINDOW = 128
D_FF = 4 * D_MODEL
ROPE_BASE = 10000.0
EPS = 1e-6
NEG = -1e30

N_PROMPT = BATCH * SEQ
N_SAMPLE = DEC_BATCH * DEC_SEQ
N_TOK = N_PROMPT + N_SAMPLE
N_COND = 8
LANES = 128
HALO = 16
MIX_TILE = 256
LAT_BLOCK = 128
LAT_KEYS = 3 * LAT_BLOCK

F32 = jnp.float32
BF16 = jnp.bfloat16

_VMEM_LIMIT = 60 * 1024 * 1024


def _params(sem):
    return pltpu.CompilerParams(dimension_semantics=sem, vmem_limit_bytes=_VMEM_LIMIT)


def _cond_group(i, tm):
    per_seq = DEC_SEQ // tm
    return jnp.maximum(i - N_PROMPT // tm + per_seq, 0) // per_seq


def _rms(x, g):
    return x * lax.rsqrt(jnp.mean(x * x, axis=-1, keepdims=True) + EPS) * g


def _ada_kernel(c_ref, w_ref, b_ref, o_ref):
    c = c_ref[...]
    s = (c * jax.nn.sigmoid(c)).astype(BF16)
    o_ref[...] = jnp.dot(s, w_ref[...].astype(BF16), preferred_element_type=F32) + b_ref[...]


def _adaln(cond, w_ada, b_ada, tn=1024):
    n_out = 6 * D_MODEL
    return pl.pallas_call(
        _ada_kernel,
        grid=(DEPTH, n_out // tn),
        in_specs=[
            pl.BlockSpec((N_COND, D_MODEL), lambda l, j: (0, 0)),
            pl.BlockSpec((None, D_MODEL, tn), lambda l, j: (l, 0, j)),
            pl.BlockSpec((None, 1, tn), lambda l, j: (l, 0, j)),
        ],
        out_specs=pl.BlockSpec((None, N_COND, tn), lambda l, j: (l, 0, j)),
        out_shape=jax.ShapeDtypeStruct((DEPTH, N_COND, n_out), F32),
        compiler_params=_params(("parallel", "parallel")),
        name="adaln",
    )(cond, w_ada, b_ada.reshape(DEPTH, 1, n_out))


def _inproj_kernel(x_ref, mod_ref, g_ref, w_ref, z_ref, h_ref, *, tn):
    y = _rms(x_ref[...], g_ref[...])
    h_ref[...] = (y * (1.0 + mod_ref[1:2, :]) + mod_ref[0:1, :]).astype(BF16)
    for n in range(0, IN_W, tn):
        z_ref[:, n:n + tn] = jnp.dot(h_ref[...], w_ref[:, n:n + tn], preferred_element_type=F32)


def _in_proj(x, mods, g_pre, w_in, tm=512, tn=512):
    return pl.pallas_call(
        functools.partial(_inproj_kernel, tn=tn),
        grid=(N_TOK // tm,),
        in_specs=[
            pl.BlockSpec((tm, D_MODEL), lambda i: (i, 0)),
            pl.BlockSpec((None, 6, D_MODEL), lambda i: (_cond_group(i, tm), 0, 0)),
            pl.BlockSpec((1, D_MODEL), lambda i: (0, 0)),
            pl.BlockSpec((D_MODEL, IN_W), lambda i: (0, 0), pipeline_mode=pl.Buffered(1)),
        ],
        out_specs=pl.BlockSpec((tm, IN_W), lambda i: (i, 0)),
        out_shape=jax.ShapeDtypeStruct((N_TOK, IN_W), F32),
        scratch_shapes=[pltpu.VMEM((tm, D_MODEL), BF16)],
        compiler_params=_params(("parallel",)),
        name="in_proj",
    )(x, mods, g_pre, w_in)


def _split_heads(x, half):
    lane = lax.broadcasted_iota(jnp.int32, x.shape, 1)
    own = jnp.where((lane < HEAD_DIM) == (half == 0), x, 0.0)
    swapped = pltpu.roll(own, HEAD_DIM, 1)
    pair = [own, swapped] if half == 0 else [swapped, own]
    return [p.astype(BF16) for p in pair]


def _dot_nt(a, b):
    return lax.dot_general(a, b, (((1,), (1,)), ((), ())), preferred_element_type=F32)


def _ctx_attn_kernel(sink_ref, q_ref, kv_ref, o_ref):
    scale = HEAD_DIM ** -0.5
    for c in range(KV_W // LANES):
        kc = kv_ref[:, c * LANES:(c + 1) * LANES]
        vc = kv_ref[:, KV_W + c * LANES:KV_W + (c + 1) * LANES]
        for hf in range(2):
            kh = 2 * c + hf
            k_at = _split_heads(kc, hf)
            v_at = _split_heads(vc, hf)
            for qc in range(2):
                j = 2 * kh + qc
                q = (q_ref[:, j * LANES:(j + 1) * LANES] * scale).astype(BF16)
                acc = None
                for half in range(2):
                    sk = sink_ref[2 * j + half]
                    s = _dot_nt(q, k_at[half])
                    m = jnp.maximum(jnp.max(s, axis=-1, keepdims=True), sk)
                    p = jnp.exp(s - m)
                    den = jnp.sum(p, axis=-1, keepdims=True) + jnp.exp(sk - m)
                    o = jnp.dot(p.astype(BF16), v_at[half], preferred_element_type=F32) / den
                    acc = o if acc is None else acc + o
                o_ref[:, j * LANES:(j + 1) * LANES] = acc.astype(BF16)


def _ctx_attention(z, sink):
    return pl.pallas_call(
        _ctx_attn_kernel,
        grid=(BATCH,),
        in_specs=[
            pl.BlockSpec(memory_space=pltpu.SMEM),
            pl.BlockSpec((SEQ, ATTN_W), lambda b: (b, 0)),
            pl.BlockSpec((SEQ, 2 * KV_W), lambda b: (b, ATTN_W // (2 * KV_W))),
        ],
        out_specs=pl.BlockSpec((SEQ, ATTN_W), lambda b: (b, 0)),
        out_shape=jax.ShapeDtypeStruct((N_PROMPT, ATTN_W), BF16),
        compiler_params=_params(("parallel",)),
        name="ctx_attn",
    )(sink, z, z)


def _rope_tables():
    quarter = HEAD_DIM // 4
    inv_freq = 1.0 / (ROPE_BASE ** (np.arange(quarter, dtype=np.float32) / quarter))
    t = np.arange(DEC_SEQ)
    row_pos = (t // GRID_W).astype(np.float32)
    col_pos = (t % GRID_W).astype(np.float32)
    d = np.arange(LANES) % HEAD_DIM
    pos = np.where(d[None, :] < HEAD_DIM // 2, row_pos[:, None], col_pos[:, None])
    ang = (pos * inv_freq[d % quarter][None, :]).astype(np.float32)
    first = (d % (2 * quarter)) < quarter
    cos = np.cos(ang)
    sin = np.sin(ang)
    sin_next = np.where(first[None, :], -sin, 0.0)
    sin_prev = np.where(first[None, :], 0.0, sin)
    return (jnp.asarray(cos, F32), jnp.asarray(sin_next, F32), jnp.asarray(sin_prev, F32))


def _rope(x, cos, sin_next, sin_prev):
    quarter = HEAD_DIM // 4
    return (x * cos + pltpu.roll(x, LANES - quarter, 1) * sin_next
            + pltpu.roll(x, quarter, 1) * sin_prev)


def _lat_attn_kernel(sink_ref, q_ref, kv_ref, ck_ref, cv_ref, cos_ref, sn_ref, sp_ref, o_ref):
    n = pl.program_id(1)
    scale = HEAD_DIM ** -0.5
    q0 = pl.multiple_of(n * LAT_BLOCK, LAT_BLOCK)
    k0 = pl.multiple_of(jnp.clip(q0 - LAT_BLOCK, 0, DEC_SEQ - LAT_KEYS), LAT_BLOCK)
    q_tab = [r[pl.ds(q0, LAT_BLOCK), :] for r in (cos_ref, sn_ref, sp_ref)]
    k_tab = [r[pl.ds(k0, LAT_KEYS), :] for r in (cos_ref, sn_ref, sp_ref)]
    qi = lax.broadcasted_iota(jnp.int32, (2 * LAT_BLOCK, LAT_KEYS), 0) & (LAT_BLOCK - 1)
    kj = lax.broadcasted_iota(jnp.int32, (2 * LAT_BLOCK, LAT_KEYS), 1)
    valid = jnp.abs((k0 + kj) - (q0 + qi)) <= WINDOW
    top = lax.broadcasted_iota(jnp.int32, (2 * LAT_BLOCK, 1), 0) < LAT_BLOCK
    for c in range(KV_W // LANES):
        kc = _rope(kv_ref[pl.ds(k0, LAT_KEYS), c * LANES:(c + 1) * LANES], *k_tab)
        vc = kv_ref[pl.ds(k0, LAT_KEYS), KV_W + c * LANES:KV_W + (c + 1) * LANES]
        ckc = ck_ref[:, c * LANES:(c + 1) * LANES]
        cvc = cv_ref[:, c * LANES:(c + 1) * LANES]
        for hf in range(2):
            kh = 2 * c + hf
            k_at = _split_heads(kc, hf)
            v_at = _split_heads(vc, hf)
            ck_at = _split_heads(ckc, hf)
            cv_at = _split_heads(cvc, hf)
            ja, jb = 2 * kh, 2 * kh + 1
            qa = _rope(q_ref[:, ja * LANES:(ja + 1) * LANES], *q_tab)
            qb = _rope(q_ref[:, jb * LANES:(jb + 1) * LANES], *q_tab)
            q2 = (jnp.concatenate([qa, qb], axis=0) * scale).astype(BF16)
            acc = None
            for half in range(2):
                sk = jnp.where(top, sink_ref[2 * ja + half], sink_ref[2 * jb + half])
                s_loc = jnp.where(valid, _dot_nt(q2, k_at[half]), NEG)
                s_ctx = _dot_nt(q2, ck_at[half])
                m = jnp.maximum(jnp.maximum(jnp.max(s_loc, axis=-1, keepdims=True),
                                            jnp.max(s_ctx, axis=-1, keepdims=True)), sk)
                p_loc = jnp.exp(s_loc - m)
                p_ctx = jnp.exp(s_ctx - m)
                den = (jnp.sum(p_loc, axis=-1, keepdims=True)
                       + jnp.sum(p_ctx, axis=-1, keepdims=True) + jnp.exp(sk - m))
                o = (jnp.dot(p_loc.astype(BF16), v_at[half], preferred_element_type=F32)
                     + jnp.dot(p_ctx.astype(BF16), cv_at[half], preferred_element_type=F32)) / den
                acc = o if acc is None else acc + o
            o_ref[:, ja * LANES:(ja + 1) * LANES] = acc[:LAT_BLOCK].astype(BF16)
            o_ref[:, jb * LANES:(jb + 1) * LANES] = acc[LAT_BLOCK:].astype(BF16)


def _lat_attention(z, sink, cache_k, cache_v, layer, tables):
    blocks = DEC_SEQ // LAT_BLOCK
    q_row0 = N_PROMPT // LAT_BLOCK
    kv_row0 = N_PROMPT // DEC_SEQ
    tab_spec = pl.BlockSpec((DEC_SEQ, LANES), lambda b, n: (0, 0))
    cache_spec = pl.BlockSpec((None, None, PAST_LEN, KV_W), lambda b, n: (b, layer, 0, 0))
    return pl.pallas_call(
        _lat_attn_kernel,
        grid=(DEC_BATCH, blocks),
        in_specs=[
            pl.BlockSpec(memory_space=pltpu.SMEM),
            pl.BlockSpec((LAT_BLOCK, ATTN_W), lambda b, n: (q_row0 + b * blocks + n, 0)),
            pl.BlockSpec((DEC_SEQ, 2 * KV_W), lambda b, n: (kv_row0 + b, ATTN_W // (2 * KV_W))),
            cache_spec, cache_spec, tab_spec, tab_spec, tab_spec,
        ],
        out_specs=pl.BlockSpec((LAT_BLOCK, ATTN_W), lambda b, n: (b * blocks + n, 0)),
        out_shape=jax.ShapeDtypeStruct((N_SAMPLE, ATTN_W), BF16),
        compiler_params=_params(("parallel", "arbitrary")),
        name="lat_attn",
    )(sink, z, z, cache_k, cache_v, *tables)


def _mixer_kernel(zc_ref, zp_ref, zn_ref, wpool_ref, pscale_ref, dw_ref, cb_ref, lng_ref, lnb_ref,
                  wpw_ref, pool_ref, conv_ref, pu_ref, hx_ref):
    i = pl.program_id(0)
    prompt_tiles = N_PROMPT // MIX_TILE
    tiles_per_seq = DEC_SEQ // MIX_TILE
    is_prompt = i < prompt_tiles
    j = (jnp.maximum(i - prompt_tiles, 0)) % tiles_per_seq
    prev_ok = jnp.logical_and(jnp.logical_not(is_prompt), j > 0)
    next_ok = jnp.logical_and(jnp.logical_not(is_prompt), j < tiles_per_seq - 1)
    t0 = jnp.where(is_prompt, 0, j * MIX_TILE)
    seq_len = jnp.where(is_prompt, SEQ, DEC_SEQ)
    cur = slice(HALO, HALO + MIX_TILE)
    nxt = slice(HALO + MIX_TILE, 2 * HALO + MIX_TILE)

    def glu(ref):
        return ref[:, POOL_W:POOL_W + CONV_W] * jax.nn.sigmoid(ref[:, POOL_W + CONV_W:])

    pu_ref[0:HALO, :] = jnp.where(prev_ok, zp_ref[:, 0:POOL_W], 0.0)
    pu_ref[cur, :] = zc_ref[:, 0:POOL_W]
    pu_ref[nxt, :] = jnp.where(next_ok, zn_ref[:, 0:POOL_W], 0.0)
    hx_ref[0:HALO, :] = jnp.where(prev_ok, glu(zp_ref), 0.0)
    hx_ref[cur, :] = glu(zc_ref)
    hx_ref[nxt, :] = jnp.where(next_ok, glu(zn_ref), 0.0)

    ts = t0 + lax.broadcasted_iota(jnp.int32, (MIX_TILE, 1), 0)
    for gi, w in enumerate(POOL_WINDOWS):
        lanes = slice(gi * POOL_GW, (gi + 1) * POOL_GW)
        wsum = None
        for k in range(w):
            v = pu_ref[pl.ds(HALO - w // 2 + k, MIX_TILE), lanes]
            wsum = v if wsum is None else wsum + v
        lo = jnp.clip(ts - w // 2, 0, seq_len)
        hi = jnp.clip(ts - w // 2 + w, 0, seq_len)
        pooled = wsum / (hi - lo).astype(F32) - pu_ref[cur, lanes]
        y = jnp.dot(pooled.astype(BF16), wpool_ref[gi], preferred_element_type=F32)
        pool_ref[:, lanes] = (y * pscale_ref[:, lanes]).astype(BF16)

    acc = None
    for k in range(CONV_K):
        v = hx_ref[pl.ds(HALO - CONV_K // 2 + k, MIX_TILE), :] * dw_ref[k:k + 1, :]
        acc = v if acc is None else acc + v
    h = acc + cb_ref[...]
    mu = jnp.mean(h, axis=-1, keepdims=True)
    var = jnp.mean(jnp.square(h - mu), axis=-1, keepdims=True)
    y = (h - mu) * lax.rsqrt(var + EPS) * lng_ref[...] + lnb_ref[...]
    y = y * jax.nn.sigmoid(y)
    conv_ref[...] = jnp.dot(y.astype(BF16), wpw_ref[...], preferred_element_type=F32).astype(BF16)


def _mixers(z, w_pool, pool_scale, conv_dw, conv_b, ln_g, ln_b, w_pw):
    tiles = N_TOK // MIX_TILE
    halo_per_tile = MIX_TILE // HALO
    last_halo = N_TOK // HALO - 1
    mix_w = POOL_W + 2 * CONV_W
    col = (IN_W - mix_w) // mix_w
    full = lambda shape: pl.BlockSpec(shape, lambda i: (0,) * len(shape))
    return pl.pallas_call(
        _mixer_kernel,
        grid=(tiles,),
        in_specs=[
            pl.BlockSpec((MIX_TILE, mix_w), lambda i: (i, col)),
            pl.BlockSpec((HALO, mix_w), lambda i: (jnp.maximum(i * halo_per_tile - 1, 0), col)),
            pl.BlockSpec((HALO, mix_w), lambda i: (jnp.minimum((i + 1) * halo_per_tile, last_halo), col)),
            full((len(POOL_WINDOWS), POOL_GW, POOL_GW)),
            full((1, POOL_W)),
            full((CONV_K, CONV_W)),
            full((1, CONV_W)), full((1, CONV_W)), full((1, CONV_W)),
            full((CONV_W, CONV_W)),
        ],
        out_specs=[pl.BlockSpec((MIX_TILE, POOL_W), lambda i: (i, 0)),
                   pl.BlockSpec((MIX_TILE, CONV_W), lambda i: (i, 0))],
        out_shape=[jax.ShapeDtypeStruct((N_TOK, POOL_W), BF16),
                   jax.ShapeDtypeStruct((N_TOK, CONV_W), BF16)],
        scratch_shapes=[pltpu.VMEM((MIX_TILE + 2 * HALO, POOL_W), F32),
                        pltpu.VMEM((MIX_TILE + 2 * HALO, CONV_W), F32)],
        compiler_params=_params(("parallel",)),
        name="mixers",
    )(z, z, z, w_pool, pool_scale, conv_dw, conv_b, ln_g, ln_b, w_pw)


def _outproj_kernel(attn_ref, pool_ref, conv_ref, x_ref, mod_ref, g_ref, w_ref, o_ref):
    o1 = ATTN_W
    o2 = ATTN_W + POOL_W
    acc = jnp.dot(attn_ref[...], w_ref[0:o1, :], preferred_element_type=F32)
    acc += jnp.dot(pool_ref[...], w_ref[o1:o2, :], preferred_element_type=F32)
    acc += jnp.dot(conv_ref[...], w_ref[o2:, :], preferred_element_type=F32)
    o_ref[...] = x_ref[...] + mod_ref[2:3, :] * _rms(acc, g_ref[...])


def _out_proj(attn, pool, conv, x, mods, g_post, w_out, tm=512):
    row = lambda i: (i, 0)
    return pl.pallas_call(
        _outproj_kernel,
        grid=(N_TOK // tm,),
        in_specs=[
            pl.BlockSpec((tm, ATTN_W), row),
            pl.BlockSpec((tm, POOL_W), row),
            pl.BlockSpec((tm, CONV_W), row),
            pl.BlockSpec((tm, D_MODEL), row),
            pl.BlockSpec((None, 6, D_MODEL), lambda i: (_cond_group(i, tm), 0, 0)),
            pl.BlockSpec((1, D_MODEL), lambda i: (0, 0)),
            pl.BlockSpec((D_MODEL, D_MODEL), lambda i: (0, 0), pipeline_mode=pl.Buffered(1)),
        ],
        out_specs=pl.BlockSpec((tm, D_MODEL), row),
        out_shape=jax.ShapeDtypeStruct((N_TOK, D_MODEL), F32),
        compiler_params=_params(("parallel",)),
        name="out_proj",
    )(attn, pool, conv, x, mods, g_post, w_out)


def _mlp_kernel(x_ref, mod_ref, gpre_ref, gpost_ref, w1_ref, w2_ref, o_ref, h_ref):
    j = pl.program_id(1)

    @pl.when(j == 0)
    def _():
        y = _rms(x_ref[...], gpre_ref[...])
        h_ref[...] = (y * (1.0 + mod_ref[4:5, :]) + mod_ref[3:4, :]).astype(BF16)
        o_ref[...] = jnp.zeros_like(o_ref)

    a = jnp.dot(h_ref[...], w1_ref[...], preferred_element_type=F32)
    a = jnp.square(jnp.maximum(a, 0.0)).astype(BF16)
    o_ref[...] += jnp.dot(a, w2_ref[...], preferred_element_type=F32)

    @pl.when(j == pl.num_programs(1) - 1)
    def _():
        o_ref[...] = x_ref[...] + mod_ref[5:6, :] * _rms(o_ref[...], gpost_ref[...])


def _mlp(x, mods, g_pre, g_post, w1, w2, tm=512, tf=1024):
    return pl.pallas_call(
        _mlp_kernel,
        grid=(N_TOK // tm, D_FF // tf),
        in_specs=[
            pl.BlockSpec((tm, D_MODEL), lambda i, j: (i, 0)),
            pl.BlockSpec((None, 6, D_MODEL), lambda i, j: (_cond_group(i, tm), 0, 0)),
            pl.BlockSpec((1, D_MODEL), lambda i, j: (0, 0)),
            pl.BlockSpec((1, D_MODEL), lambda i, j: (0, 0)),
            pl.BlockSpec((D_MODEL, tf), lambda i, j: (0, j)),
            pl.BlockSpec((tf, D_MODEL), lambda i, j: (j, 0)),
        ],
        out_specs=pl.BlockSpec((tm, D_MODEL), lambda i, j: (i, 0)),
        out_shape=jax.ShapeDtypeStruct((N_TOK, D_MODEL), F32),
        scratch_shapes=[pltpu.VMEM((tm, D_MODEL), BF16)],
        compiler_params=_params(("parallel", "arbitrary")),
        name="mlp",
    )(x, mods, g_pre, g_post, w1, w2)


def kernel(x_prompt, x_sample, cache_k, cache_v, c, c_ctx, w_ada, b_ada, g_pre1, g_post1, g_pre2,
           g_post2, w_in, sink, w_pool, pool_scale, conv_dw, conv_b, conv_ln_g, conv_ln_b,
           w_conv_pw, w_out, w_mlp1, w_mlp2):
    cond = jnp.concatenate([c_ctx[None, :], c, jnp.zeros((N_COND - 1 - DEC_BATCH, D_MODEL), F32)], axis=0)
    mods = _adaln(cond, w_ada, b_ada).reshape(DEPTH, N_COND, 6, D_MODEL)
    x = jnp.concatenate([x_prompt.reshape(N_PROMPT, D_MODEL), x_sample.reshape(N_SAMPLE, D_MODEL)], axis=0)
    ck = cache_k.reshape(DEC_BATCH, DEPTH, PAST_LEN, KV_W)
    cv = cache_v.reshape(DEC_BATCH, DEPTH, PAST_LEN, KV_W)
    tables = _rope_tables()
    w_in_b, w_out_b = w_in.astype(BF16), w_out.astype(BF16)
    w1_b, w2_b = w_mlp1.astype(BF16), w_mlp2.astype(BF16)
    w_pool_b, w_pw_b = w_pool.astype(BF16), w_conv_pw.astype(BF16)
    vec = lambda a, l: a[l][None, :]
    new_k, new_v = [], []
    for l in range(DEPTH):
        z = _in_proj(x, mods[l], vec(g_pre1, l), w_in_b[l])
        new_k.append(z[:N_PROMPT, ATTN_W:ATTN_W + KV_W].reshape(BATCH, SEQ, N_KV_HEADS, HEAD_DIM))
        new_v.append(z[:N_PROMPT, ATTN_W + KV_W:ATTN_W + 2 * KV_W].reshape(BATCH, SEQ, N_KV_HEADS, HEAD_DIM))
        attn = jnp.concatenate([_ctx_attention(z, sink[l]),
                                _lat_attention(z, sink[l], ck, cv, l, tables)], axis=0)
        pool, conv = _mixers(z, w_pool_b[l], vec(pool_scale, l), conv_dw[l], vec(conv_b, l),
                             vec(conv_ln_g, l), vec(conv_ln_b, l), w_pw_b[l])
        x = _out_proj(attn, pool, conv, x, mods[l], vec(g_post1, l), w_out_b[l])
        x = _mlp(x, mods[l], vec(g_pre2, l), vec(g_post2, l), w1_b[l], w2_b[l])
    y_prompt = x[:N_PROMPT].reshape(BATCH, SEQ, D_MODEL)
    y_sample = x[N_PROMPT:].reshape(DEC_BATCH, DEC_SEQ, D_MODEL)
    return (y_prompt, y_sample, jnp.stack(new_k, axis=1), jnp.stack(new_v, axis=1))
```

```python
import functools

import numpy as np
import jax
import jax.numpy as jnp
from jax import lax
from jax.experimental import pallas as pl
from jax.experimental.pallas import tpu as pltpu

D_MODEL = 2048
BATCH = 32
SEQ = 256
DEPTH = 4
DEC_BATCH = 2
DEC_SEQ = 2048
PAST_LEN = 512
GRID_W = 64
HEAD_DIM = 64
ATTN_W = 1024
N_HEADS = 16
N_KV_HEADS = 4
KV_W = 256
POOL_W = 512
POOL_WINDOWS = (2, 4, 8, 16)
POOL_GW = 128
CONV_W = 512
CONV_K = 31
IN_W = 3072
WINDOW = 128
D_FF = 4 * D_MODEL
ROPE_BASE = 10000.0
EPS = 1e-6
NEG = -1e30

N_PROMPT = BATCH * SEQ
N_SAMPLE = DEC_BATCH * DEC_SEQ
N_TOK = N_PROMPT + N_SAMPLE
N_COND = 8
LANES = 128
HALO = 16
MIX_TILE = 256
SUBLANES = 8
ROT_ROWS = MIX_TILE + HALO + SUBLANES
LAT_BLOCK = 128
LAT_KEYS = 3 * LAT_BLOCK

F32 = jnp.float32
BF16 = jnp.bfloat16

_VMEM_LIMIT = 60 * 1024 * 1024


def _params(sem):
    return pltpu.CompilerParams(dimension_semantics=sem, vmem_limit_bytes=_VMEM_LIMIT)


def _cond_group(i, tm):
    per_seq = DEC_SEQ // tm
    return jnp.maximum(i - N_PROMPT // tm + per_seq, 0) // per_seq


def _rms(x, g):
    return x * lax.rsqrt(jnp.mean(x * x, axis=-1, keepdims=True) + EPS) * g


def _ada_kernel(c_ref, w_ref, b_ref, o_ref):
    c = c_ref[...]
    s = (c * jax.nn.sigmoid(c)).astype(BF16)
    o_ref[...] = jnp.dot(s, w_ref[...].astype(BF16), preferred_element_type=F32) + b_ref[...]


def _adaln(cond, w_ada, b_ada, tn=1024):
    n_out = 6 * D_MODEL
    return pl.pallas_call(
        _ada_kernel,
        grid=(DEPTH, n_out // tn),
        in_specs=[
            pl.BlockSpec((N_COND, D_MODEL), lambda l, j: (0, 0)),
            pl.BlockSpec((None, D_MODEL, tn), lambda l, j: (l, 0, j)),
            pl.BlockSpec((None, 1, tn), lambda l, j: (l, 0, j)),
        ],
        out_specs=pl.BlockSpec((None, N_COND, tn), lambda l, j: (l, 0, j)),
        out_shape=jax.ShapeDtypeStruct((DEPTH, N_COND, n_out), F32),
        compiler_params=_params(("parallel", "parallel")),
        name="adaln",
    )(cond, w_ada, b_ada.reshape(DEPTH, 1, n_out))


def _pair_specs(tm, width, grid_rank=1):
    n = N_PROMPT // tm
    if grid_rank == 1:
        return [pl.BlockSpec((tm, width), lambda i: (jnp.minimum(i, n - 1), 0)),
                pl.BlockSpec((tm, width), lambda i: (jnp.maximum(i - n, 0), 0))]
    return [pl.BlockSpec((tm, width), lambda i, j: (jnp.minimum(i, n - 1), 0)),
            pl.BlockSpec((tm, width), lambda i, j: (jnp.maximum(i - n, 0), 0))]


def _pick(i, tm, prompt_ref, latent_ref):
    return jnp.where(i < N_PROMPT // tm, prompt_ref[...], latent_ref[...])


def _inproj_kernel(*refs, tm, tn, split_x):
    nx = 2 if split_x else 1
    x_refs = refs[:nx]
    mod_ref, g_ref, w_ref = refs[nx:nx + 3]
    z_ref, k_ref, v_ref, h_ref = refs[-4:]
    i = pl.program_id(0)
    x = _pick(i, tm, *x_refs) if split_x else x_refs[0][...]
    y = _rms(x, g_ref[...])
    h_ref[...] = (y * (1.0 + mod_ref[1:2, :]) + mod_ref[0:1, :]).astype(BF16)
    for n in range(0, IN_W, tn):
        z_ref[:, n:n + tn] = jnp.dot(h_ref[...], w_ref[:, n:n + tn], preferred_element_type=F32)

    @pl.when(i < N_PROMPT // tm)
    def _():
        for s in range(tm // SEQ):
            rows = slice(s * SEQ, (s + 1) * SEQ)
            k_ref[s] = z_ref[rows, ATTN_W:ATTN_W + KV_W]
            v_ref[s] = z_ref[rows, ATTN_W + KV_W:ATTN_W + 2 * KV_W]


def _in_proj(x, mods, g_pre, w_in, layer, kv_bufs, tm=512, tn=512):
    split_x = isinstance(x, tuple)
    n = N_PROMPT // tm
    x_specs = _pair_specs(tm, D_MODEL) if split_x else [pl.BlockSpec((tm, D_MODEL), lambda i: (i, 0))]
    x_args = list(x) if split_x else [x]
    kv_spec = pl.BlockSpec((tm // SEQ, None, SEQ, KV_W), lambda i: (jnp.minimum(i, n - 1), layer, 0, 0))
    kv_shape = jax.ShapeDtypeStruct((BATCH, DEPTH, SEQ, KV_W), F32)
    n_in = len(x_args) + 3
    alias_specs, alias_args, aliases = [], [], {}
    if kv_bufs is not None:
        alias_specs = [pl.BlockSpec(memory_space=pl.ANY)] * 2
        alias_args = list(kv_bufs)
        aliases = {n_in: 1, n_in + 1: 2}
    return pl.pallas_call(
        functools.partial(_inproj_kernel, tm=tm, tn=tn, split_x=split_x),
        grid=(N_TOK // tm,),
        in_specs=x_specs + [
            pl.BlockSpec((None, 6, D_MODEL), lambda i: (_cond_group(i, tm), 0, 0)),
            pl.BlockSpec((1, D_MODEL), lambda i: (0, 0)),
            pl.BlockSpec((D_MODEL, IN_W), lambda i: (0, 0), pipeline_mode=pl.Buffered(1)),
        ] + alias_specs,
        out_specs=[pl.BlockSpec((tm, IN_W), lambda i: (i, 0)), kv_spec, kv_spec],
        out_shape=[jax.ShapeDtypeStruct((N_TOK, IN_W), F32), kv_shape, kv_shape],
        scratch_shapes=[pltpu.VMEM((tm, D_MODEL), BF16)],
        input_output_aliases=aliases,
        compiler_params=_params(("arbitrary",)),
        name="in_proj",
    )(*x_args, mods, g_pre, w_in, *alias_args)


def _split_heads(x, half):
    lane = lax.broadcasted_iota(jnp.int32, x.shape, 1)
    own = jnp.where((lane < HEAD_DIM) == (half == 0), x, 0.0)
    swapped = pltpu.roll(own, HEAD_DIM, 1)
    pair = [own, swapped] if half == 0 else [swapped, own]
    return [p.astype(BF16) for p in pair]


def _dot_nt(a, b):
    return lax.dot_general(a, b, (((1,), (1,)), ((), ())), preferred_element_type=F32)


def _ctx_attn_kernel(sink_ref, q_ref, kv_ref, o_ref):
    scale = HEAD_DIM ** -0.5
    for c in range(KV_W // LANES):
        kc = kv_ref[:, c * LANES:(c + 1) * LANES]
        vc = kv_ref[:, KV_W + c * LANES:KV_W + (c + 1) * LANES]
        for hf in range(2):
            kh = 2 * c + hf
            k_at = _split_heads(kc, hf)
            v_at = _split_heads(vc, hf)
            for qc in range(2):
                j = 2 * kh + qc
                q = (q_ref[:, j * LANES:(j + 1) * LANES] * scale).astype(BF16)
                acc = None
                for half in range(2):
                    sk = sink_ref[2 * j + half]
                    s = _dot_nt(q, k_at[half])
                    m = jnp.maximum(jnp.max(s, axis=-1, keepdims=True), sk)
                    p = jnp.exp(s - m)
                    den = jnp.sum(p, axis=-1, keepdims=True) + jnp.exp(sk - m)
                    o = jnp.dot(p.astype(BF16), v_at[half], preferred_element_type=F32) / den
                    acc = o if acc is None else acc + o
                o_ref[:, j * LANES:(j + 1) * LANES] = acc.astype(BF16)


def _ctx_attention(z, sink):
    return pl.pallas_call(
        _ctx_attn_kernel,
        grid=(BATCH,),
        in_specs=[
            pl.BlockSpec(memory_space=pltpu.SMEM),
            pl.BlockSpec((SEQ, ATTN_W), lambda b: (b, 0)),
            pl.BlockSpec((SEQ, 2 * KV_W), lambda b: (b, ATTN_W // (2 * KV_W))),
        ],
        out_specs=pl.BlockSpec((SEQ, ATTN_W), lambda b: (b, 0)),
        out_shape=jax.ShapeDtypeStruct((N_PROMPT, ATTN_W), BF16),
        compiler_params=_params(("parallel",)),
        name="ctx_attn",
    )(sink, z, z)


def _rope_tables():
    quarter = HEAD_DIM // 4
    inv_freq = 1.0 / (ROPE_BASE ** (np.arange(quarter, dtype=np.float32) / quarter))
    t = np.arange(DEC_SEQ)
    row_pos = (t // GRID_W).astype(np.float32)
    col_pos = (t % GRID_W).astype(np.float32)
    d = np.arange(LANES) % HEAD_DIM
    pos = np.where(d[None, :] < HEAD_DIM // 2, row_pos[:, None], col_pos[:, None])
    ang = (pos * inv_freq[d % quarter][None, :]).astype(np.float32)
    first = (d % (2 * quarter)) < quarter
    cos = np.cos(ang)
    sin = np.sin(ang)
    sin_next = np.where(first[None, :], -sin, 0.0)
    sin_prev = np.where(first[None, :], 0.0, sin)
    return (jnp.asarray(cos, F32), jnp.asarray(sin_next, F32), jnp.asarray(sin_prev, F32))


def _rope(x, cos, sin_next, sin_prev):
    quarter = HEAD_DIM // 4
    return (x * cos + pltpu.roll(x, LANES - quarter, 1) * sin_next
            + pltpu.roll(x, quarter, 1) * sin_prev)


def _lat_attn_kernel(sink_ref, q_ref, kv_ref, ck_ref, cv_ref, cos_ref, sn_ref, sp_ref, o_ref):
    n = pl.program_id(1)
    scale = HEAD_DIM ** -0.5
    q0 = pl.multiple_of(n * LAT_BLOCK, LAT_BLOCK)
    k0 = pl.multiple_of(jnp.clip(q0 - LAT_BLOCK, 0, DEC_SEQ - LAT_KEYS), LAT_BLOCK)
    q_tab = [r[pl.ds(q0, LAT_BLOCK), :] for r in (cos_ref, sn_ref, sp_ref)]
    k_tab = [r[pl.ds(k0, LAT_KEYS), :] for r in (cos_ref, sn_ref, sp_ref)]
    qi = lax.broadcasted_iota(jnp.int32, (2 * LAT_BLOCK, LAT_KEYS), 0) & (LAT_BLOCK - 1)
    kj = lax.broadcasted_iota(jnp.int32, (2 * LAT_BLOCK, LAT_KEYS), 1)
    valid = jnp.abs((k0 + kj) - (q0 + qi)) <= WINDOW
    top = lax.broadcasted_iota(jnp.int32, (2 * LAT_BLOCK, 1), 0) < LAT_BLOCK
    for c in range(KV_W // LANES):
        kc = _rope(kv_ref[pl.ds(k0, LAT_KEYS), c * LANES:(c + 1) * LANES], *k_tab)
        vc = kv_ref[pl.ds(k0, LAT_KEYS), KV_W + c * LANES:KV_W + (c + 1) * LANES]
        ckc = ck_ref[:, c * LANES:(c + 1) * LANES]
        cvc = cv_ref[:, c * LANES:(c + 1) * LANES]
        for hf in range(2):
            kh = 2 * c + hf
            k_at = _split_heads(kc, hf)
            v_at = _split_heads(vc, hf)
            ck_at = _split_heads(ckc, hf)
            cv_at = _split_heads(cvc, hf)
            ja, jb = 2 * kh, 2 * kh + 1
            qa = _rope(q_ref[:, ja * LANES:(ja + 1) * LANES], *q_tab)
            qb = _rope(q_ref[:, jb * LANES:(jb + 1) * LANES], *q_tab)
            q2 = (jnp.concatenate([qa, qb], axis=0) * scale).astype(BF16)
            acc = None
            for half in range(2):
                sk = jnp.where(top, sink_ref[2 * ja + half], sink_ref[2 * jb + half])
                s_loc = jnp.where(valid, _dot_nt(q2, k_at[half]), NEG)
                s_ctx = _dot_nt(q2, ck_at[half])
                m = jnp.maximum(jnp.maximum(jnp.max(s_loc, axis=-1, keepdims=True),
                                            jnp.max(s_ctx, axis=-1, keepdims=True)), sk)
                p_loc = jnp.exp(s_loc - m)
                p_ctx = jnp.exp(s_ctx - m)
                den = (jnp.sum(p_loc, axis=-1, keepdims=True)
                       + jnp.sum(p_ctx, axis=-1, keepdims=True) + jnp.exp(sk - m))
                o = (jnp.dot(p_loc.astype(BF16), v_at[half], preferred_element_type=F32)
                     + jnp.dot(p_ctx.astype(BF16), cv_at[half], preferred_element_type=F32)) / den
                acc = o if acc is None else acc + o
            o_ref[:, ja * LANES:(ja + 1) * LANES] = acc[:LAT_BLOCK].astype(BF16)
            o_ref[:, jb * LANES:(jb + 1) * LANES] = acc[LAT_BLOCK:].astype(BF16)


def _lat_attention(z, sink, cache_k, cache_v, layer, tables):
    blocks = DEC_SEQ // LAT_BLOCK
    q_row0 = N_PROMPT // LAT_BLOCK
    kv_row0 = N_PROMPT // DEC_SEQ
    tab_spec = pl.BlockSpec((DEC_SEQ, LANES), lambda b, n: (0, 0))
    cache_spec = pl.BlockSpec((None, None, PAST_LEN, KV_W), lambda b, n: (b, layer, 0, 0))
    return pl.pallas_call(
        _lat_attn_kernel,
        grid=(DEC_BATCH, blocks),
        in_specs=[
            pl.BlockSpec(memory_space=pltpu.SMEM),
            pl.BlockSpec((LAT_BLOCK, ATTN_W), lambda b, n: (q_row0 + b * blocks + n, 0)),
            pl.BlockSpec((DEC_SEQ, 2 * KV_W), lambda b, n: (kv_row0 + b, ATTN_W // (2 * KV_W))),
            cache_spec, cache_spec, tab_spec, tab_spec, tab_spec,
        ],
        out_specs=pl.BlockSpec((LAT_BLOCK, ATTN_W), lambda b, n: (b * blocks + n, 0)),
        out_shape=jax.ShapeDtypeStruct((N_SAMPLE, ATTN_W), BF16),
        compiler_params=_params(("parallel", "arbitrary")),
        name="lat_attn",
    )(sink, z, z, cache_k, cache_v, *tables)


def _mixer_kernel(zc_ref, zp_ref, zn_ref, wpool_ref, pscale_ref, dw_ref, cb_ref, lng_ref, lnb_ref,
                  wpw_ref, pool_ref, conv_ref, pu_ref, hx_ref, rot_ref):
    i = pl.program_id(0)
    prompt_tiles = N_PROMPT // MIX_TILE
    tiles_per_seq = DEC_SEQ // MIX_TILE
    is_prompt = i < prompt_tiles
    j = (jnp.maximum(i - prompt_tiles, 0)) % tiles_per_seq
    prev_ok = jnp.logical_and(jnp.logical_not(is_prompt), j > 0)
    next_ok = jnp.logical_and(jnp.logical_not(is_prompt), j < tiles_per_seq - 1)
    t0 = jnp.where(is_prompt, 0, j * MIX_TILE)
    seq_len = jnp.where(is_prompt, SEQ, DEC_SEQ)
    cur = slice(HALO, HALO + MIX_TILE)
    nxt = slice(HALO + MIX_TILE, 2 * HALO + MIX_TILE)

    def glu(ref):
        return ref[:, POOL_W:POOL_W + CONV_W] * jax.nn.sigmoid(ref[:, POOL_W + CONV_W:])

    pu_ref[0:HALO, :] = jnp.where(prev_ok, zp_ref[:, 0:POOL_W], 0.0)
    pu_ref[cur, :] = zc_ref[:, 0:POOL_W]
    pu_ref[nxt, :] = jnp.where(next_ok, zn_ref[:, 0:POOL_W], 0.0)
    hx_ref[0:HALO, :] = jnp.where(prev_ok, glu(zp_ref), 0.0)
    hx_ref[cur, :] = glu(zc_ref)
    hx_ref[nxt, :] = jnp.where(next_ok, glu(zn_ref), 0.0)

    ts = t0 + lax.broadcasted_iota(jnp.int32, (MIX_TILE, 1), 0)
    for gi, w in enumerate(POOL_WINDOWS):
        lanes = slice(gi * POOL_GW, (gi + 1) * POOL_GW)
        wsum = None
        for k in range(w):
            v = pu_ref[pl.ds(HALO - w // 2 + k, MIX_TILE), lanes]
            wsum = v if wsum is None else wsum + v
        lo = jnp.clip(ts - w // 2, 0, seq_len)
        hi = jnp.clip(ts - w // 2 + w, 0, seq_len)
        pooled = wsum / (hi - lo).astype(F32) - pu_ref[cur, lanes]
        y = jnp.dot(pooled.astype(BF16), wpool_ref[gi], preferred_element_type=F32)
        pool_ref[:, lanes] = (y * pscale_ref[:, lanes]).astype(BF16)

    ext_rows = MIX_TILE + 2 * HALO
    hx = hx_ref[...]
    for r in range(1, SUBLANES):
        rot_ref[r - 1] = pltpu.roll(hx, ext_rows - r, 0)[:ROT_ROWS]
    acc = None
    for k in range(CONV_K):
        a, r = divmod(HALO - CONV_K // 2 + k, SUBLANES)
        src = hx_ref if r == 0 else rot_ref.at[r - 1]
        v = src[pl.ds(SUBLANES * a, MIX_TILE), :] * dw_ref[k:k + 1, :]
        acc = v if acc is None else acc + v
    h = acc + cb_ref[...]
    mu = jnp.mean(h, axis=-1, keepdims=True)
    var = jnp.mean(jnp.square(h - mu), axis=-1, keepdims=True)
    y = (h - mu) * lax.rsqrt(var + EPS) * lng_ref[...] + lnb_ref[...]
    y = y * jax.nn.sigmoid(y)
    conv_ref[...] = jnp.dot(y.astype(BF16), wpw_ref[...], preferred_element_type=F32).astype(BF16)


def _mixers(z, w_pool, pool_scale, conv_dw, conv_b, ln_g, ln_b, w_pw):
    tiles = N_TOK // MIX_TILE
    halo_per_tile = MIX_TILE // HALO
    last_halo = N_TOK // HALO - 1
    mix_w = POOL_W + 2 * CONV_W
    col = (IN_W - mix_w) // mix_w
    full = lambda shape: pl.BlockSpec(shape, lambda i: (0,) * len(shape))
    return pl.pallas_call(
        _mixer_kernel,
        grid=(tiles,),
        in_specs=[
            pl.BlockSpec((MIX_TILE, mix_w), lambda i: (i, col)),
            pl.BlockSpec((HALO, mix_w), lambda i: (jnp.maximum(i * halo_per_tile - 1, 0), col)),
            pl.BlockSpec((HALO, mix_w), lambda i: (jnp.minimum((i + 1) * halo_per_tile, last_halo), col)),
            full((len(POOL_WINDOWS), POOL_GW, POOL_GW)),
            full((1, POOL_W)),
            full((CONV_K, CONV_W)),
            full((1, CONV_W)), full((1, CONV_W)), full((1, CONV_W)),
            full((CONV_W, CONV_W)),
        ],
        out_specs=[pl.BlockSpec((MIX_TILE, POOL_W), lambda i: (i, 0)),
                   pl.BlockSpec((MIX_TILE, CONV_W), lambda i: (i, 0))],
        out_shape=[jax.ShapeDtypeStruct((N_TOK, POOL_W), BF16),
                   jax.ShapeDtypeStruct((N_TOK, CONV_W), BF16)],
        scratch_shapes=[pltpu.VMEM((MIX_TILE + 2 * HALO, POOL_W), F32),
                        pltpu.VMEM((MIX_TILE + 2 * HALO, CONV_W), F32),
                        pltpu.VMEM((SUBLANES - 1, ROT_ROWS, CONV_W), F32)],
        compiler_params=_params(("parallel",)),
        name="mixers",
    )(z, z, z, w_pool, pool_scale, conv_dw, conv_b, ln_g, ln_b, w_pw)


def _outproj_kernel(*refs, tm, split_x):
    actx_ref, alat_ref, pool_ref, conv_ref = refs[:4]
    x_refs = refs[4:-4]
    mod_ref, g_ref, w_ref, o_ref = refs[-4:]
    i = pl.program_id(0)
    x = _pick(i, tm, *x_refs) if split_x else x_refs[0][...]
    o1 = ATTN_W
    o2 = ATTN_W + POOL_W
    acc = jnp.dot(_pick(i, tm, actx_ref, alat_ref), w_ref[0:o1, :], preferred_element_type=F32)
    acc += jnp.dot(pool_ref[...], w_ref[o1:o2, :], preferred_element_type=F32)
    acc += jnp.dot(conv_ref[...], w_ref[o2:, :], preferred_element_type=F32)
    o_ref[...] = x + mod_ref[2:3, :] * _rms(acc, g_ref[...])


def _out_proj(attn_ctx, attn_lat, pool, conv, x, mods, g_post, w_out, tm=512):
    split_x = isinstance(x, tuple)
    row = lambda i: (i, 0)
    x_specs = _pair_specs(tm, D_MODEL) if split_x else [pl.BlockSpec((tm, D_MODEL), row)]
    x_args = list(x) if split_x else [x]
    return pl.pallas_call(
        functools.partial(_outproj_kernel, tm=tm, split_x=split_x),
        grid=(N_TOK // tm,),
        in_specs=_pair_specs(tm, ATTN_W) + [
            pl.BlockSpec((tm, POOL_W), row),
            pl.BlockSpec((tm, CONV_W), row),
        ] + x_specs + [
            pl.BlockSpec((None, 6, D_MODEL), lambda i: (_cond_group(i, tm), 0, 0)),
            pl.BlockSpec((1, D_MODEL), lambda i: (0, 0)),
            pl.BlockSpec((D_MODEL, D_MODEL), lambda i: (0, 0), pipeline_mode=pl.Buffered(1)),
        ],
        out_specs=pl.BlockSpec((tm, D_MODEL), row),
        out_shape=jax.ShapeDtypeStruct((N_TOK, D_MODEL), F32),
        compiler_params=_params(("parallel",)),
        name="out_proj",
    )(attn_ctx, attn_lat, pool, conv, *x_args, mods, g_post, w_out)


def _mlp_kernel(*refs, tm, split_out):
    x_ref, mod_ref, gpre_ref, gpost_ref, w1_ref, w2_ref = refs[:6]
    if split_out:
        yp_ref, ys_ref, h_ref, acc_ref = refs[6:]
    else:
        o_ref, h_ref = refs[6:]
        acc_ref = o_ref
    i = pl.program_id(0)
    j = pl.program_id(1)
    last = pl.num_programs(1) - 1

    @pl.when(j == 0)
    def _():
        y = _rms(x_ref[...], gpre_ref[...])
        h_ref[...] = (y * (1.0 + mod_ref[4:5, :]) + mod_ref[3:4, :]).astype(BF16)
        acc_ref[...] = jnp.zeros_like(acc_ref)

    a = jnp.dot(h_ref[...], w1_ref[...], preferred_element_type=F32)
    a = jnp.square(jnp.maximum(a, 0.0)).astype(BF16)
    acc_ref[...] += jnp.dot(a, w2_ref[...], preferred_element_type=F32)

    def result():
        return x_ref[...] + mod_ref[5:6, :] * _rms(acc_ref[...], gpost_ref[...])

    if split_out:
        is_prompt = i < N_PROMPT // tm

        @pl.when(jnp.logical_and(j == last, is_prompt))
        def _():
            yp_ref[...] = result()

        @pl.when(jnp.logical_and(j == last, jnp.logical_not(is_prompt)))
        def _():
            ys_ref[...] = result()
    else:
        @pl.when(j == last)
        def _():
            o_ref[...] = result()


def _mlp(x, mods, g_pre, g_post, w1, w2, split_out=False, tm=512, tf=1024):
    if split_out:
        out_specs = _pair_specs(tm, D_MODEL, grid_rank=2)
        out_shape = [jax.ShapeDtypeStruct((N_PROMPT, D_MODEL), F32),
                     jax.ShapeDtypeStruct((N_SAMPLE, D_MODEL), F32)]
        scratch = [pltpu.VMEM((tm, D_MODEL), BF16), pltpu.VMEM((tm, D_MODEL), F32)]
    else:
        out_specs = pl.BlockSpec((tm, D_MODEL), lambda i, j: (i, 0))
        out_shape = jax.ShapeDtypeStruct((N_TOK, D_MODEL), F32)
        scratch = [pltpu.VMEM((tm, D_MODEL), BF16)]
    return pl.pallas_call(
        functools.partial(_mlp_kernel, tm=tm, split_out=split_out),
        grid=(N_TOK // tm, D_FF // tf),
        in_specs=[
            pl.BlockSpec((tm, D_MODEL), lambda i, j: (i, 0)),
            pl.BlockSpec((None, 6, D_MODEL), lambda i, j: (_cond_group(i, tm), 0, 0)),
            pl.BlockSpec((1, D_MODEL), lambda i, j: (0, 0)),
            pl.BlockSpec((1, D_MODEL), lambda i, j: (0, 0)),
            pl.BlockSpec((D_MODEL, tf), lambda i, j: (0, j)),
            pl.BlockSpec((tf, D_MODEL), lambda i, j: (j, 0)),
        ],
        out_specs=out_specs,
        out_shape=out_shape,
        scratch_shapes=scratch,
        compiler_params=_params(("arbitrary", "arbitrary")),
        name="mlp",
    )(x, mods, g_pre, g_post, w1, w2)


def kernel(x_prompt, x_sample, cache_k, cache_v, c, c_ctx, w_ada, b_ada, g_pre1, g_post1, g_pre2,
           g_post2, w_in, sink, w_pool, pool_scale, conv_dw, conv_b, conv_ln_g, conv_ln_b,
           w_conv_pw, w_out, w_mlp1, w_mlp2):
    cond = jnp.concatenate([c_ctx[None, :], c, jnp.zeros((N_COND - 1 - DEC_BATCH, D_MODEL), F32)], axis=0)
    mods = _adaln(cond, w_ada, b_ada).reshape(DEPTH, N_COND, 6, D_MODEL)
    x = (x_prompt.reshape(N_PROMPT, D_MODEL), x_sample.reshape(N_SAMPLE, D_MODEL))
    ck = cache_k.reshape(DEC_BATCH, DEPTH, PAST_LEN, KV_W)
    cv = cache_v.reshape(DEC_BATCH, DEPTH, PAST_LEN, KV_W)
    tables = _rope_tables()
    w_in_b, w_out_b = w_in.astype(BF16), w_out.astype(BF16)
    w1_b, w2_b = w_mlp1.astype(BF16), w_mlp2.astype(BF16)
    w_pool_b, w_pw_b = w_pool.astype(BF16), w_conv_pw.astype(BF16)
    vec = lambda a, l: a[l][None, :]
    kv_bufs = None
    for l in range(DEPTH):
        z, new_k, new_v = _in_proj(x, mods[l], vec(g_pre1, l), w_in_b[l], l, kv_bufs)
        kv_bufs = (new_k, new_v)
        attn_ctx = _ctx_attention(z, sink[l])
        attn_lat = _lat_attention(z, sink[l], ck, cv, l, tables)
        pool, conv = _mixers(z, w_pool_b[l], vec(pool_scale, l), conv_dw[l], vec(conv_b, l),
                             vec(conv_ln_g, l), vec(conv_ln_b, l), w_pw_b[l])
        x = _out_proj(attn_ctx, attn_lat, pool, conv, x, mods[l], vec(g_post1, l), w_out_b[l])
        x = _mlp(x, mods[l], vec(g_pre2, l), vec(g_post2, l), w1_b[l], w2_b[l],
                 split_out=(l == DEPTH - 1))
    kv_shape = (BATCH, DEPTH, SEQ, N_KV_HEADS, HEAD_DIM)
    return (x[0].reshape(BATCH, SEQ, D_MODEL), x[1].reshape(DEC_BATCH, DEC_SEQ, D_MODEL),
            new_k.reshape(kv_shape), new_v.reshape(kv_shape))
```

```python
import functools

import numpy as np
import jax
import jax.numpy as jnp
from jax import lax
from jax.experimental import pallas as pl
from jax.experimental.pallas import tpu as pltpu

D_MODEL = 2048
BATCH = 32
SEQ = 256
DEPTH = 4
DEC_BATCH = 2
DEC_SEQ = 2048
PAST_LEN = 512
GRID_W = 64
HEAD_DIM = 64
ATTN_W = 1024
N_HEADS = 16
N_KV_HEADS = 4
KV_W = 256
POOL_W = 512
POOL_WINDOWS = (2, 4, 8, 16)
POOL_GW = 128
CONV_W = 512
CONV_K = 31
IN_W = 3072
WINDOW = 128
D_FF = 4 * D_MODEL
ROPE_BASE = 10000.0
EPS = 1e-6
NEG = -1e30

N_PROMPT = BATCH * SEQ
N_SAMPLE = DEC_BATCH * DEC_SEQ
N_TOK = N_PROMPT + N_SAMPLE
N_COND = 8
LANES = 128
HALO = 16
MIX_TILE = 256
SUBLANES = 8
ROT_ROWS = MIX_TILE + HALO + SUBLANES
CONV_ROWS = 64
LAT_BLOCK = 128
LAT_KEYS = 3 * LAT_BLOCK

F32 = jnp.float32
BF16 = jnp.bfloat16

_VMEM_LIMIT = 60 * 1024 * 1024


def _params(sem):
    return pltpu.CompilerParams(dimension_semantics=sem, vmem_limit_bytes=_VMEM_LIMIT)


def _cond_group(i, tm):
    per_seq = DEC_SEQ // tm
    return jnp.maximum(i - N_PROMPT // tm + per_seq, 0) // per_seq


def _rms(x, g):
    return x * lax.rsqrt(jnp.mean(x * x, axis=-1, keepdims=True) + EPS) * g


def _ada_kernel(c_ref, w_ref, b_ref, o_ref):
    c = c_ref[...]
    s = (c * jax.nn.sigmoid(c)).astype(BF16)
    o_ref[...] = jnp.dot(s, w_ref[...].astype(BF16), preferred_element_type=F32) + b_ref[...]


def _adaln(cond, w_ada, b_ada, tn=1024):
    n_out = 6 * D_MODEL
    return pl.pallas_call(
        _ada_kernel,
        grid=(DEPTH, n_out // tn),
        in_specs=[
            pl.BlockSpec((N_COND, D_MODEL), lambda l, j: (0, 0)),
            pl.BlockSpec((None, D_MODEL, tn), lambda l, j: (l, 0, j)),
            pl.BlockSpec((None, 1, tn), lambda l, j: (l, 0, j)),
        ],
        out_specs=pl.BlockSpec((None, N_COND, tn), lambda l, j: (l, 0, j)),
        out_shape=jax.ShapeDtypeStruct((DEPTH, N_COND, n_out), F32),
        compiler_params=_params(("parallel", "parallel")),
        name="adaln",
    )(cond, w_ada, b_ada.reshape(DEPTH, 1, n_out))


def _pair_specs(tm, width, grid_rank=1, lag=0):
    n = N_PROMPT // tm
    m = N_SAMPLE // tm
    prompt = lambda i: (jnp.clip(i - lag, 0, n - 1), 0)
    latent = lambda i: (jnp.clip(i - lag - n, 0, m - 1), 0)
    if grid_rank == 1:
        return [pl.BlockSpec((tm, width), prompt), pl.BlockSpec((tm, width), latent)]
    return [pl.BlockSpec((tm, width), lambda i, j: prompt(i)),
            pl.BlockSpec((tm, width), lambda i, j: latent(i))]


def _inproj_kernel(*refs, tm, tn, sub, split_x):
    nx = 2 if split_x else 1
    x_refs = refs[:nx]
    mod_ref, g_ref, w_ref = refs[nx:nx + 3]
    z_ref, k_ref, v_ref, h_ref = refs[-4:]
    i = pl.program_id(0)
    is_prompt = i < N_PROMPT // tm
    for r in range(0, tm, sub):
        rows = slice(r, r + sub)
        if split_x:
            x = jnp.where(is_prompt, x_refs[0][rows, :], x_refs[1][rows, :])
        else:
            x = x_refs[0][rows, :]
        y = _rms(x, g_ref[...])
        h_ref[rows, :] = (y * (1.0 + mod_ref[1:2, :]) + mod_ref[0:1, :]).astype(BF16)
        for n in range(0, IN_W, tn):
            z_ref[rows, n:n + tn] = jnp.dot(h_ref[rows, :], w_ref[:, n:n + tn],
                                            preferred_element_type=F32)

    @pl.when(is_prompt)
    def _():
        for s in range(tm // SEQ):
            rows = slice(s * SEQ, (s + 1) * SEQ)
            k_ref[s] = z_ref[rows, ATTN_W:ATTN_W + KV_W]
            v_ref[s] = z_ref[rows, ATTN_W + KV_W:ATTN_W + 2 * KV_W]


def _in_proj(x, mods, g_pre, w_in, layer, kv_bufs, tm=512, tn=512, sub=256):
    split_x = isinstance(x, tuple)
    n = N_PROMPT // tm
    x_specs = _pair_specs(tm, D_MODEL) if split_x else [pl.BlockSpec((tm, D_MODEL), lambda i: (i, 0))]
    x_args = list(x) if split_x else [x]
    kv_spec = pl.BlockSpec((tm // SEQ, None, SEQ, KV_W), lambda i: (jnp.minimum(i, n - 1), layer, 0, 0))
    kv_shape = jax.ShapeDtypeStruct((BATCH, DEPTH, SEQ, KV_W), F32)
    n_in = len(x_args) + 3
    alias_specs, alias_args, aliases = [], [], {}
    if kv_bufs is not None:
        alias_specs = [pl.BlockSpec(memory_space=pl.ANY)] * 2
        alias_args = list(kv_bufs)
        aliases = {n_in: 1, n_in + 1: 2}
    return pl.pallas_call(
        functools.partial(_inproj_kernel, tm=tm, tn=tn, sub=sub, split_x=split_x),
        grid=(N_TOK // tm,),
        in_specs=x_specs + [
            pl.BlockSpec((None, 6, D_MODEL), lambda i: (_cond_group(i, tm), 0, 0)),
            pl.BlockSpec((1, D_MODEL), lambda i: (0, 0)),
            pl.BlockSpec((None, D_MODEL, IN_W), lambda i: (layer, 0, 0), pipeline_mode=pl.Buffered(1)),
        ] + alias_specs,
        out_specs=[pl.BlockSpec((tm, IN_W), lambda i: (i, 0)), kv_spec, kv_spec],
        out_shape=[jax.ShapeDtypeStruct((N_TOK, IN_W), F32), kv_shape, kv_shape],
        scratch_shapes=[pltpu.VMEM((tm, D_MODEL), BF16)],
        input_output_aliases=aliases,
        compiler_params=_params(("arbitrary",)),
        name="in_proj",
    )(*x_args, mods, g_pre, w_in, *alias_args)


def _split_heads(x, half):
    lane = lax.broadcasted_iota(jnp.int32, x.shape, 1)
    own = jnp.where((lane < HEAD_DIM) == (half == 0), x, 0.0)
    swapped = pltpu.roll(own, HEAD_DIM, 1)
    pair = [own, swapped] if half == 0 else [swapped, own]
    return [p.astype(BF16) for p in pair]


def _dot_nt(a, b):
    return lax.dot_general(a, b, (((1,), (1,)), ((), ())), preferred_element_type=F32)


def _ctx_attn_kernel(sink_ref, q_ref, kv_ref, o_ref):
    scale = HEAD_DIM ** -0.5
    for c in range(KV_W // LANES):
        kc = kv_ref[:, c * LANES:(c + 1) * LANES]
        vc = kv_ref[:, KV_W + c * LANES:KV_W + (c + 1) * LANES]
        for hf in range(2):
            kh = 2 * c + hf
            k_at = _split_heads(kc, hf)
            v_at = _split_heads(vc, hf)
            for qc in range(2):
                j = 2 * kh + qc
                q = (q_ref[:, j * LANES:(j + 1) * LANES] * scale).astype(BF16)
                acc = None
                for half in range(2):
                    sk = sink_ref[2 * j + half]
                    s = _dot_nt(q, k_at[half])
                    m = jnp.maximum(jnp.max(s, axis=-1, keepdims=True), sk)
                    p = jnp.exp(s - m)
                    den = jnp.sum(p, axis=-1, keepdims=True) + jnp.exp(sk - m)
                    o = jnp.dot(p.astype(BF16), v_at[half], preferred_element_type=F32) / den
                    acc = o if acc is None else acc + o
                o_ref[:, j * LANES:(j + 1) * LANES] = acc.astype(BF16)


def _ctx_attention(z, sink):
    return pl.pallas_call(
        _ctx_attn_kernel,
        grid=(BATCH,),
        in_specs=[
            pl.BlockSpec(memory_space=pltpu.SMEM),
            pl.BlockSpec((SEQ, ATTN_W), lambda b: (b, 0)),
            pl.BlockSpec((SEQ, 2 * KV_W), lambda b: (b, ATTN_W // (2 * KV_W))),
        ],
        out_specs=pl.BlockSpec((SEQ, ATTN_W), lambda b: (b, 0)),
        out_shape=jax.ShapeDtypeStruct((N_PROMPT, ATTN_W), BF16),
        compiler_params=_params(("parallel",)),
        name="ctx_attn",
    )(sink, z, z)


def _rope_tables():
    quarter = HEAD_DIM // 4
    inv_freq = 1.0 / (ROPE_BASE ** (np.arange(quarter, dtype=np.float32) / quarter))
    t = np.arange(DEC_SEQ)
    row_pos = (t // GRID_W).astype(np.float32)
    col_pos = (t % GRID_W).astype(np.float32)
    d = np.arange(LANES) % HEAD_DIM
    pos = np.where(d[None, :] < HEAD_DIM // 2, row_pos[:, None], col_pos[:, None])
    ang = (pos * inv_freq[d % quarter][None, :]).astype(np.float32)
    first = (d % (2 * quarter)) < quarter
    cos = np.cos(ang)
    sin = np.sin(ang)
    sin_next = np.where(first[None, :], -sin, 0.0)
    sin_prev = np.where(first[None, :], 0.0, sin)
    return (jnp.asarray(cos, F32), jnp.asarray(sin_next, F32), jnp.asarray(sin_prev, F32))


def _rope(x, cos, sin_next, sin_prev):
    quarter = HEAD_DIM // 4
    return (x * cos + pltpu.roll(x, LANES - quarter, 1) * sin_next
            + pltpu.roll(x, quarter, 1) * sin_prev)


def _lat_attn_kernel(sink_ref, q_ref, kv_ref, ck_ref, cv_ref, cos_ref, sn_ref, sp_ref, o_ref):
    n = pl.program_id(1)
    scale = HEAD_DIM ** -0.5
    q0 = pl.multiple_of(n * LAT_BLOCK, LAT_BLOCK)
    k0 = pl.multiple_of(jnp.clip(q0 - LAT_BLOCK, 0, DEC_SEQ - LAT_KEYS), LAT_BLOCK)
    q_tab = [r[pl.ds(q0, LAT_BLOCK), :] for r in (cos_ref, sn_ref, sp_ref)]
    k_tab = [r[pl.ds(k0, LAT_KEYS), :] for r in (cos_ref, sn_ref, sp_ref)]
    qi = lax.broadcasted_iota(jnp.int32, (2 * LAT_BLOCK, LAT_KEYS), 0) & (LAT_BLOCK - 1)
    kj = lax.broadcasted_iota(jnp.int32, (2 * LAT_BLOCK, LAT_KEYS), 1)
    valid = jnp.abs((k0 + kj) - (q0 + qi)) <= WINDOW
    top = lax.broadcasted_iota(jnp.int32, (2 * LAT_BLOCK, 1), 0) < LAT_BLOCK
    for c in range(KV_W // LANES):
        kc = _rope(kv_ref[pl.ds(k0, LAT_KEYS), c * LANES:(c + 1) * LANES], *k_tab)
        vc = kv_ref[pl.ds(k0, LAT_KEYS), KV_W + c * LANES:KV_W + (c + 1) * LANES]
        ckc = ck_ref[:, c * LANES:(c + 1) * LANES]
        cvc = cv_ref[:, c * LANES:(c + 1) * LANES]
        for hf in range(2):
            kh = 2 * c + hf
            k_at = _split_heads(kc, hf)
            v_at = _split_heads(vc, hf)
            ck_at = _split_heads(ckc, hf)
            cv_at = _split_heads(cvc, hf)
            ja, jb = 2 * kh, 2 * kh + 1
            qa = _rope(q_ref[:, ja * LANES:(ja + 1) * LANES], *q_tab)
            qb = _rope(q_ref[:, jb * LANES:(jb + 1) * LANES], *q_tab)
            q2 = (jnp.concatenate([qa, qb], axis=0) * scale).astype(BF16)
            acc = None
            for half in range(2):
                sk = jnp.where(top, sink_ref[2 * ja + half], sink_ref[2 * jb + half])
                s_loc = jnp.where(valid, _dot_nt(q2, k_at[half]), NEG)
                s_ctx = _dot_nt(q2, ck_at[half])
                m = jnp.maximum(jnp.maximum(jnp.max(s_loc, axis=-1, keepdims=True),
                                            jnp.max(s_ctx, axis=-1, keepdims=True)), sk)
                p_loc = jnp.exp(s_loc - m)
                p_ctx = jnp.exp(s_ctx - m)
                den = (jnp.sum(p_loc, axis=-1, keepdims=True)
                       + jnp.sum(p_ctx, axis=-1, keepdims=True) + jnp.exp(sk - m))
                o = (jnp.dot(p_loc.astype(BF16), v_at[half], preferred_element_type=F32)
                     + jnp.dot(p_ctx.astype(BF16), cv_at[half], preferred_element_type=F32)) / den
                acc = o if acc is None else acc + o
            o_ref[:, ja * LANES:(ja + 1) * LANES] = acc[:LAT_BLOCK].astype(BF16)
            o_ref[:, jb * LANES:(jb + 1) * LANES] = acc[LAT_BLOCK:].astype(BF16)


def _lat_attention(z, sink, cache_k, cache_v, layer, tables):
    blocks = DEC_SEQ // LAT_BLOCK
    q_row0 = N_PROMPT // LAT_BLOCK
    kv_row0 = N_PROMPT // DEC_SEQ
    tab_spec = pl.BlockSpec((DEC_SEQ, LANES), lambda b, n: (0, 0))
    cache_spec = pl.BlockSpec((None, None, PAST_LEN, KV_W), lambda b, n: (b, layer, 0, 0))
    return pl.pallas_call(
        _lat_attn_kernel,
        grid=(DEC_BATCH, blocks),
        in_specs=[
            pl.BlockSpec(memory_space=pltpu.SMEM),
            pl.BlockSpec((LAT_BLOCK, ATTN_W), lambda b, n: (q_row0 + b * blocks + n, 0)),
            pl.BlockSpec((DEC_SEQ, 2 * KV_W), lambda b, n: (kv_row0 + b, ATTN_W // (2 * KV_W))),
            cache_spec, cache_spec, tab_spec, tab_spec, tab_spec,
        ],
        out_specs=pl.BlockSpec((LAT_BLOCK, ATTN_W), lambda b, n: (b * blocks + n, 0)),
        out_shape=jax.ShapeDtypeStruct((N_SAMPLE, ATTN_W), BF16),
        compiler_params=_params(("parallel", "arbitrary")),
        name="lat_attn",
    )(sink, z, z, cache_k, cache_v, *tables)


def _mix_tile(cur, prev, nxt, prev_ok, next_ok, t0, seq_len, weights, scratch):
    wpool_ref, pscale_ref, dw_ref, cb_ref, lng_ref, lnb_ref, wpw_ref = weights
    pu_ref, hx_ref, rot_ref, cv_ref = scratch
    mid = slice(HALO, HALO + MIX_TILE)
    end = slice(HALO + MIX_TILE, 2 * HALO + MIX_TILE)

    def pu(src):
        ref, rows = src
        return ref[rows, 0:POOL_W]

    def glu(src):
        ref, rows = src
        return ref[rows, POOL_W:POOL_W + CONV_W] * jax.nn.sigmoid(ref[rows, POOL_W + CONV_W:])

    pu_ref[0:HALO, :] = jnp.where(prev_ok, pu(prev), 0.0)
    pu_ref[mid, :] = pu(cur)
    pu_ref[end, :] = jnp.where(next_ok, pu(nxt), 0.0)
    hx_ref[0:HALO, :] = jnp.where(prev_ok, glu(prev), 0.0)
    hx_ref[mid, :] = glu(cur)
    hx_ref[end, :] = jnp.where(next_ok, glu(nxt), 0.0)

    ts = t0 + lax.broadcasted_iota(jnp.int32, (MIX_TILE, 1), 0)
    pool = []
    for gi, w in enumerate(POOL_WINDOWS):
        lanes = slice(gi * POOL_GW, (gi + 1) * POOL_GW)
        wsum = None
        for k in range(w):
            v = pu_ref[pl.ds(HALO - w // 2 + k, MIX_TILE), lanes]
            wsum = v if wsum is None else wsum + v
        lo = jnp.clip(ts - w // 2, 0, seq_len)
        hi = jnp.clip(ts - w // 2 + w, 0, seq_len)
        pooled = wsum / (hi - lo).astype(F32) - pu_ref[mid, lanes]
        y = jnp.dot(pooled.astype(BF16), wpool_ref[gi], preferred_element_type=F32)
        pool.append((y * pscale_ref[:, lanes]).astype(BF16))

    ext_rows = MIX_TILE + 2 * HALO
    hx = hx_ref[...]
    for r in range(1, SUBLANES):
        rot_ref[r - 1] = pltpu.roll(hx, ext_rows - r, 0)[:ROT_ROWS]
    for c in range(0, CONV_W, LANES):
        lanes = slice(c, c + LANES)
        for rb in range(0, MIX_TILE, CONV_ROWS):
            acc = None
            for k in range(CONV_K):
                a, r = divmod(HALO - CONV_K // 2 + k, SUBLANES)
                src = hx_ref if r == 0 else rot_ref.at[r - 1]
                v = src[pl.ds(SUBLANES * a + rb, CONV_ROWS), lanes] * dw_ref[k:k + 1, lanes]
                acc = v if acc is None else acc + v
            cv_ref[rb:rb + CONV_ROWS, lanes] = acc
    h = cv_ref[...] + cb_ref[...]
    mu = jnp.mean(h, axis=-1, keepdims=True)
    var = jnp.mean(jnp.square(h - mu), axis=-1, keepdims=True)
    y = (h - mu) * lax.rsqrt(var + EPS) * lng_ref[...] + lnb_ref[...]
    y = y * jax.nn.sigmoid(y)
    conv = jnp.dot(y.astype(BF16), wpw_ref[...], preferred_element_type=F32).astype(BF16)
    return jnp.concatenate(pool, axis=-1), conv


def _mixout_kernel(*refs, tm, split_x):
    zc_ref, zp_ref, zn_ref, actx_ref, alat_ref = refs[:5]
    x_refs = refs[5:-16]
    mod_ref, g_ref, wo_ref = refs[-16:-13]
    weights = refs[-13:-6]
    o_ref = refs[-6]
    scratch = refs[-5:-1]
    pc_ref = refs[-1]
    i = pl.program_id(0)
    n_tiles = N_TOK // tm
    prompt_tiles = N_PROMPT // tm
    tiles_per_seq = DEC_SEQ // tm
    n_sub = tm // MIX_TILE
    o1 = ATTN_W
    o2 = ATTN_W + POOL_W

    @pl.when(i == 0)
    def _():
        pc_ref[...] = jnp.zeros_like(pc_ref)

    out_is_prompt = jnp.maximum(i - 1, 0) < prompt_tiles
    for s in range(n_sub):
        rows = slice(s * MIX_TILE, (s + 1) * MIX_TILE)
        attn = jnp.where(out_is_prompt, actx_ref[rows, :], alat_ref[rows, :])
        mix = jnp.concatenate([attn, pc_ref[rows, :]], axis=-1)
        acc = jnp.dot(mix, wo_ref[...], preferred_element_type=F32)
        if split_x:
            x = jnp.where(out_is_prompt, x_refs[0][rows, :], x_refs[1][rows, :])
        else:
            x = x_refs[0][rows, :]
        o_ref[rows, :] = x + mod_ref[2:3, :] * _rms(acc, g_ref[...])

    t = jnp.minimum(i, n_tiles - 1)
    is_prompt = t < prompt_tiles
    is_latent = jnp.logical_not(is_prompt)
    j = jnp.maximum(t - prompt_tiles, 0) % tiles_per_seq
    seq_len = jnp.where(is_prompt, SEQ, DEC_SEQ)
    for s in range(n_sub):
        rows = slice(s * MIX_TILE, (s + 1) * MIX_TILE)
        if s == 0:
            prev, prev_ok = (zp_ref, slice(0, HALO)), jnp.logical_and(is_latent, j > 0)
        else:
            prev, prev_ok = (zc_ref, slice(s * MIX_TILE - HALO, s * MIX_TILE)), is_latent
        if s == n_sub - 1:
            nxt, next_ok = (zn_ref, slice(0, HALO)), jnp.logical_and(is_latent, j < tiles_per_seq - 1)
        else:
            nxt, next_ok = (zc_ref, slice((s + 1) * MIX_TILE, (s + 1) * MIX_TILE + HALO)), is_latent
        t0 = jnp.where(is_prompt, 0, j * tm + s * MIX_TILE)
        pool, conv = _mix_tile((zc_ref, rows), prev, nxt, prev_ok, next_ok, t0, seq_len, weights,
                               scratch)
        pc_ref[rows, 0:POOL_W] = pool
        pc_ref[rows, POOL_W:] = conv


def _mix_out(z, attn_ctx, attn_lat, x, mods, g_post, w_out, layer, w_pool, pool_scale, conv_dw,
             conv_b, ln_g, ln_b, w_pw, tm=512):
    split_x = isinstance(x, tuple)
    n_tiles = N_TOK // tm
    halo_per_tile = tm // HALO
    last_halo = N_TOK // HALO - 1
    mix_w = POOL_W + 2 * CONV_W
    col = (IN_W - mix_w) // mix_w
    full = lambda shape: pl.BlockSpec(shape, lambda i: (0,) * len(shape))
    mix_tile = lambda i: jnp.minimum(i, n_tiles - 1)
    out_row = lambda i: (jnp.maximum(i - 1, 0), 0)
    x_specs = (_pair_specs(tm, D_MODEL, lag=1) if split_x else [pl.BlockSpec((tm, D_MODEL), out_row)])
    x_args = list(x) if split_x else [x]
    return pl.pallas_call(
        functools.partial(_mixout_kernel, tm=tm, split_x=split_x),
        grid=(n_tiles + 1,),
        in_specs=[
            pl.BlockSpec((tm, mix_w), lambda i: (mix_tile(i), col)),
            pl.BlockSpec((HALO, mix_w), lambda i: (jnp.maximum(mix_tile(i) * halo_per_tile - 1, 0), col)),
            pl.BlockSpec((HALO, mix_w),
                         lambda i: (jnp.minimum((mix_tile(i) + 1) * halo_per_tile, last_halo), col)),
        ] + _pair_specs(tm, ATTN_W, lag=1) + x_specs + [
            pl.BlockSpec((None, 6, D_MODEL), lambda i: (_cond_group(jnp.maximum(i - 1, 0), tm), 0, 0)),
            pl.BlockSpec((1, D_MODEL), lambda i: (0, 0)),
            pl.BlockSpec((None, D_MODEL, D_MODEL), lambda i: (layer, 0, 0), pipeline_mode=pl.Buffered(1)),
            full((len(POOL_WINDOWS), POOL_GW, POOL_GW)),
            full((1, POOL_W)),
            full((CONV_K, CONV_W)),
            full((1, CONV_W)), full((1, CONV_W)), full((1, CONV_W)),
            full((CONV_W, CONV_W)),
        ],
        out_specs=pl.BlockSpec((tm, D_MODEL), out_row),
        out_shape=jax.ShapeDtypeStruct((N_TOK, D_MODEL), F32),
        scratch_shapes=[pltpu.VMEM((MIX_TILE + 2 * HALO, POOL_W), F32),
                        pltpu.VMEM((MIX_TILE + 2 * HALO, CONV_W), F32),
                        pltpu.VMEM((SUBLANES - 1, ROT_ROWS, CONV_W), F32),
                        pltpu.VMEM((MIX_TILE, CONV_W), F32),
                        pltpu.VMEM((tm, POOL_W + CONV_W), BF16)],
        compiler_params=_params(("arbitrary",)),
        name="mix_out",
    )(z, z, z, attn_ctx, attn_lat, *x_args, mods, g_post, w_out, w_pool, pool_scale, conv_dw,
      conv_b, ln_g, ln_b, w_pw)


def _mlp_kernel(*refs, tm, sub, split_out):
    x_ref, mod_ref, gpre_ref, gpost_ref, w1_ref, w2_ref = refs[:6]
    if split_out:
        yp_ref, ys_ref, h_ref, acc_ref = refs[6:]
    else:
        o_ref, h_ref = refs[6:]
        acc_ref = o_ref
    i = pl.program_id(0)
    j = pl.program_id(1)
    last = pl.num_programs(1) - 1

    def step(first, out_ref):
        for r in range(0, tm, sub):
            rows = slice(r, r + sub)
            if first:
                y = _rms(x_ref[rows, :], gpre_ref[...])
                h = (y * (1.0 + mod_ref[4:5, :]) + mod_ref[3:4, :]).astype(BF16)
                h_ref[rows, :] = h
            else:
                h = h_ref[rows, :]
            a = jnp.dot(h, w1_ref[...], preferred_element_type=F32)
            a = jnp.square(jnp.maximum(a, 0.0)).astype(BF16)
            acc = jnp.dot(a, w2_ref[...], preferred_element_type=F32)
            if not first:
                acc += acc_ref[rows, :]
            if out_ref is None:
                acc_ref[rows, :] = acc
            else:
                out_ref[rows, :] = x_ref[rows, :] + mod_ref[5:6, :] * _rms(acc, gpost_ref[...])

    pl.when(j == 0)(lambda: step(True, None))
    pl.when(jnp.logical_and(j > 0, j < last))(lambda: step(False, None))
    if split_out:
        is_prompt = i < N_PROMPT // tm
        pl.when(jnp.logical_and(j == last, is_prompt))(lambda: step(False, yp_ref))
        pl.when(jnp.logical_and(j == last, jnp.logical_not(is_prompt)))(lambda: step(False, ys_ref))
    else:
        pl.when(j == last)(lambda: step(False, o_ref))


def _mlp(x, mods, g_pre, g_post, w1, w2, layer, split_out=False, tm=512, tf=1024, sub=256):
    if split_out:
        out_specs = _pair_specs(tm, D_MODEL, grid_rank=2)
        out_shape = [jax.ShapeDtypeStruct((N_PROMPT, D_MODEL), F32),
                     jax.ShapeDtypeStruct((N_SAMPLE, D_MODEL), F32)]
        scratch = [pltpu.VMEM((tm, D_MODEL), BF16), pltpu.VMEM((tm, D_MODEL), F32)]
    else:
        out_specs = pl.BlockSpec((tm, D_MODEL), lambda i, j: (i, 0))
        out_shape = jax.ShapeDtypeStruct((N_TOK, D_MODEL), F32)
        scratch = [pltpu.VMEM((tm, D_MODEL), BF16)]
    return pl.pallas_call(
        functools.partial(_mlp_kernel, tm=tm, sub=sub, split_out=split_out),
        grid=(N_TOK // tm, D_FF // tf),
        in_specs=[
            pl.BlockSpec((tm, D_MODEL), lambda i, j: (i, 0)),
            pl.BlockSpec((None, 6, D_MODEL), lambda i, j: (_cond_group(i, tm), 0, 0)),
            pl.BlockSpec((1, D_MODEL), lambda i, j: (0, 0)),
            pl.BlockSpec((1, D_MODEL), lambda i, j: (0, 0)),
            pl.BlockSpec((None, D_MODEL, tf), lambda i, j: (layer, 0, j)),
            pl.BlockSpec((None, tf, D_MODEL), lambda i, j: (layer, j, 0)),
        ],
        out_specs=out_specs,
        out_shape=out_shape,
        scratch_shapes=scratch,
        compiler_params=_params(("arbitrary", "arbitrary")),
        name="mlp",
    )(x, mods, g_pre, g_post, w1, w2)


def kernel(x_prompt, x_sample, cache_k, cache_v, c, c_ctx, w_ada, b_ada, g_pre1, g_post1, g_pre2,
           g_post2, w_in, sink, w_pool, pool_scale, conv_dw, conv_b, conv_ln_g, conv_ln_b,
           w_conv_pw, w_out, w_mlp1, w_mlp2):
    cond = jnp.concatenate([c_ctx[None, :], c, jnp.zeros((N_COND - 1 - DEC_BATCH, D_MODEL), F32)], axis=0)
    mods = _adaln(cond, w_ada, b_ada).reshape(DEPTH, N_COND, 6, D_MODEL)
    x = (x_prompt.reshape(N_PROMPT, D_MODEL), x_sample.reshape(N_SAMPLE, D_MODEL))
    ck = cache_k.reshape(DEC_BATCH, DEPTH, PAST_LEN, KV_W)
    cv = cache_v.reshape(DEC_BATCH, DEPTH, PAST_LEN, KV_W)
    tables = _rope_tables()
    w_in_b, w_out_b = w_in.astype(BF16), w_out.astype(BF16)
    w1_b, w2_b = w_mlp1.astype(BF16), w_mlp2.astype(BF16)
    w_pool_b, w_pw_b = w_pool.astype(BF16), w_conv_pw.astype(BF16)
    vec = lambda a, l: a[l][None, :]
    kv_bufs = None
    for l in range(DEPTH):
        z, new_k, new_v = _in_proj(x, mods[l], vec(g_pre1, l), w_in_b, l, kv_bufs)
        kv_bufs = (new_k, new_v)
        attn_ctx = _ctx_attention(z, sink[l])
        attn_lat = _lat_attention(z, sink[l], ck, cv, l, tables)
        x = _mix_out(z, attn_ctx, attn_lat, x, mods[l], vec(g_post1, l), w_out_b, l, w_pool_b[l],
                     vec(pool_scale, l), conv_dw[l], vec(conv_b, l), vec(conv_ln_g, l),
                     vec(conv_ln_b, l), w_pw_b[l])
        x = _mlp(x, mods[l], vec(g_pre2, l), vec(g_post2, l), w1_b, w2_b, l,
                 split_out=(l == DEPTH - 1))
    kv_shape = (BATCH, DEPTH, SEQ, N_KV_HEADS, HEAD_DIM)
    return (x[0].reshape(BATCH, SEQ, D_MODEL), x[1].reshape(DEC_BATCH, DEC_SEQ, D_MODEL),
            new_k.reshape(kv_shape), new_v.reshape(kv_shape))
```

```python
import functools

import numpy as np
import jax
import jax.numpy as jnp
from jax import lax
from jax.experimental import pallas as pl
from jax.experimental.pallas import tpu as pltpu

D_MODEL = 2048
BATCH = 32
SEQ = 256
DEPTH = 4
DEC_BATCH = 2
DEC_SEQ = 2048
PAST_LEN = 512
GRID_W = 64
HEAD_DIM = 64
ATTN_W = 1024
N_HEADS = 16
N_KV_HEADS = 4
KV_W = 256
POOL_W = 512
POOL_WINDOWS = (2, 4, 8, 16)
POOL_GW = 128
CONV_W = 512
CONV_K = 31
IN_W = 3072
WINDOW = 128
D_FF = 4 * D_MODEL
ROPE_BASE = 10000.0
EPS = 1e-6
NEG = -1e30

N_PROMPT = BATCH * SEQ
N_SAMPLE = DEC_BATCH * DEC_SEQ
N_TOK = N_PROMPT + N_SAMPLE
N_COND = 8
LANES = 128
HALO = 16
MIX_TILE = 256
SUBLANES = 8
ROT_ROWS = MIX_TILE + HALO + SUBLANES
CONV_ROWS = 64
LAT_BLOCK = 128
LAT_KEYS = 3 * LAT_BLOCK

F32 = jnp.float32
BF16 = jnp.bfloat16

_VMEM_LIMIT = 60 * 1024 * 1024


def _params(sem):
    return pltpu.CompilerParams(dimension_semantics=sem, vmem_limit_bytes=_VMEM_LIMIT)


def _cond_group(i, tm):
    per_seq = DEC_SEQ // tm
    return jnp.maximum(i - N_PROMPT // tm + per_seq, 0) // per_seq


def _rms(x, g):
    return x * lax.rsqrt(jnp.mean(x * x, axis=-1, keepdims=True) + EPS) * g


def _ada_kernel(c_ref, w_ref, b_ref, o_ref):
    c = c_ref[...]
    s = (c * jax.nn.sigmoid(c)).astype(BF16)
    o_ref[...] = jnp.dot(s, w_ref[...].astype(BF16), preferred_element_type=F32) + b_ref[...]


def _adaln(cond, w_ada, b_ada, tn=1024):
    n_out = 6 * D_MODEL
    return pl.pallas_call(
        _ada_kernel,
        grid=(DEPTH, n_out // tn),
        in_specs=[
            pl.BlockSpec((N_COND, D_MODEL), lambda l, j: (0, 0)),
            pl.BlockSpec((None, D_MODEL, tn), lambda l, j: (l, 0, j)),
            pl.BlockSpec((None, 1, tn), lambda l, j: (l, 0, j)),
        ],
        out_specs=pl.BlockSpec((None, N_COND, tn), lambda l, j: (l, 0, j)),
        out_shape=jax.ShapeDtypeStruct((DEPTH, N_COND, n_out), F32),
        compiler_params=_params(("parallel", "parallel")),
        name="adaln",
    )(cond, w_ada, b_ada.reshape(DEPTH, 1, n_out))


def _pair_specs(tm, width, grid_rank=1, lag=0):
    n = N_PROMPT // tm
    m = N_SAMPLE // tm
    prompt = lambda i: (jnp.clip(i - lag, 0, n - 1), 0)
    latent = lambda i: (jnp.clip(i - lag - n, 0, m - 1), 0)
    if grid_rank == 1:
        return [pl.BlockSpec((tm, width), prompt), pl.BlockSpec((tm, width), latent)]
    return [pl.BlockSpec((tm, width), lambda i, j: prompt(i)),
            pl.BlockSpec((tm, width), lambda i, j: latent(i))]


def _inproj_kernel(*refs, tm, tn, sub, split_x):
    nx = 2 if split_x else 1
    x_refs = refs[:nx]
    mod_ref, g_ref, w_ref = refs[nx:nx + 3]
    z_ref, k_ref, v_ref, h_ref = refs[-4:]
    i = pl.program_id(0)
    is_prompt = i < N_PROMPT // tm
    for r in range(0, tm, sub):
        rows = slice(r, r + sub)
        if split_x:
            x = jnp.where(is_prompt, x_refs[0][rows, :], x_refs[1][rows, :])
        else:
            x = x_refs[0][rows, :]
        y = _rms(x, g_ref[...])
        h_ref[rows, :] = (y * (1.0 + mod_ref[1:2, :]) + mod_ref[0:1, :]).astype(BF16)
        for n in range(0, IN_W, tn):
            z_ref[rows, n:n + tn] = jnp.dot(h_ref[rows, :], w_ref[:, n:n + tn],
                                            preferred_element_type=F32)

    @pl.when(is_prompt)
    def _():
        for s in range(tm // SEQ):
            rows = slice(s * SEQ, (s + 1) * SEQ)
            k_ref[s] = z_ref[rows, ATTN_W:ATTN_W + KV_W]
            v_ref[s] = z_ref[rows, ATTN_W + KV_W:ATTN_W + 2 * KV_W]


def _in_proj(x, mods, g_pre, w_in, layer, kv_bufs, tm=512, tn=512, sub=512):
    split_x = isinstance(x, tuple)
    n = N_PROMPT // tm
    x_specs = _pair_specs(tm, D_MODEL) if split_x else [pl.BlockSpec((tm, D_MODEL), lambda i: (i, 0))]
    x_args = list(x) if split_x else [x]
    kv_spec = pl.BlockSpec((tm // SEQ, None, SEQ, KV_W), lambda i: (jnp.minimum(i, n - 1), layer, 0, 0))
    kv_shape = jax.ShapeDtypeStruct((BATCH, DEPTH, SEQ, KV_W), F32)
    n_in = len(x_args) + 3
    alias_specs = [pl.BlockSpec(memory_space=pl.ANY)] * 2
    alias_args = list(kv_bufs)
    aliases = {n_in: 1, n_in + 1: 2}
    return pl.pallas_call(
        functools.partial(_inproj_kernel, tm=tm, tn=tn, sub=sub, split_x=split_x),
        grid=(N_TOK // tm,),
        in_specs=x_specs + [
            pl.BlockSpec((None, 6, D_MODEL), lambda i: (_cond_group(i, tm), 0, 0)),
            pl.BlockSpec((1, D_MODEL), lambda i: (0, 0)),
            pl.BlockSpec((None, D_MODEL, IN_W), lambda i: (layer, 0, 0), pipeline_mode=pl.Buffered(1)),
        ] + alias_specs,
        out_specs=[pl.BlockSpec((tm, IN_W), lambda i: (i, 0)), kv_spec, kv_spec],
        out_shape=[jax.ShapeDtypeStruct((N_TOK, IN_W), F32), kv_shape, kv_shape],
        scratch_shapes=[pltpu.VMEM((tm, D_MODEL), BF16)],
        input_output_aliases=aliases,
        compiler_params=_params(("arbitrary",)),
        name="in_proj",
    )(*x_args, mods, g_pre, w_in, *alias_args)


def _split_heads(x, half):
    lane = lax.broadcasted_iota(jnp.int32, x.shape, 1)
    own = jnp.where((lane < HEAD_DIM) == (half == 0), x, 0.0)
    swapped = pltpu.roll(own, HEAD_DIM, 1)
    pair = [own, swapped] if half == 0 else [swapped, own]
    return [p.astype(BF16) for p in pair]


def _dot_nt(a, b):
    return lax.dot_general(a, b, (((1,), (1,)), ((), ())), preferred_element_type=F32)


def _ctx_attn_kernel(sink_ref, q_ref, kv_ref, o_ref):
    scale = HEAD_DIM ** -0.5
    for c in range(KV_W // LANES):
        kc = kv_ref[:, c * LANES:(c + 1) * LANES]
        vc = kv_ref[:, KV_W + c * LANES:KV_W + (c + 1) * LANES]
        for hf in range(2):
            kh = 2 * c + hf
            k_at = _split_heads(kc, hf)
            v_at = _split_heads(vc, hf)
            for qc in range(2):
                j = 2 * kh + qc
                q = (q_ref[:, j * LANES:(j + 1) * LANES] * scale).astype(BF16)
                acc = None
                for half in range(2):
                    sk = sink_ref[2 * j + half]
                    s = _dot_nt(q, k_at[half])
                    m = jnp.maximum(jnp.max(s, axis=-1, keepdims=True), sk)
                    p = jnp.exp(s - m)
                    den = jnp.sum(p, axis=-1, keepdims=True) + jnp.exp(sk - m)
                    o = jnp.dot(p.astype(BF16), v_at[half], preferred_element_type=F32) / den
                    acc = o if acc is None else acc + o
                o_ref[:, j * LANES:(j + 1) * LANES] = acc.astype(BF16)


def _ctx_attention(z, sink):
    return pl.pallas_call(
        _ctx_attn_kernel,
        grid=(BATCH,),
        in_specs=[
            pl.BlockSpec(memory_space=pltpu.SMEM),
            pl.BlockSpec((SEQ, ATTN_W), lambda b: (b, 0)),
            pl.BlockSpec((SEQ, 2 * KV_W), lambda b: (b, ATTN_W // (2 * KV_W))),
        ],
        out_specs=pl.BlockSpec((SEQ, ATTN_W), lambda b: (b, 0)),
        out_shape=jax.ShapeDtypeStruct((N_PROMPT, ATTN_W), BF16),
        compiler_params=_params(("parallel",)),
        name="ctx_attn",
    )(sink, z, z)


def _rope_tables():
    quarter = HEAD_DIM // 4
    inv_freq = 1.0 / (ROPE_BASE ** (np.arange(quarter, dtype=np.float32) / quarter))
    t = np.arange(DEC_SEQ)
    row_pos = (t // GRID_W).astype(np.float32)
    col_pos = (t % GRID_W).astype(np.float32)
    d = np.arange(LANES) % HEAD_DIM
    pos = np.where(d[None, :] < HEAD_DIM // 2, row_pos[:, None], col_pos[:, None])
    ang = (pos * inv_freq[d % quarter][None, :]).astype(np.float32)
    first = (d % (2 * quarter)) < quarter
    cos = np.cos(ang)
    sin = np.sin(ang)
    sin_next = np.where(first[None, :], -sin, 0.0)
    sin_prev = np.where(first[None, :], 0.0, sin)
    return (jnp.asarray(cos, F32), jnp.asarray(sin_next, F32), jnp.asarray(sin_prev, F32))


def _rope(x, cos, sin_next, sin_prev):
    quarter = HEAD_DIM // 4
    return (x * cos + pltpu.roll(x, LANES - quarter, 1) * sin_next
            + pltpu.roll(x, quarter, 1) * sin_prev)


def _lat_attn_kernel(sink_ref, q_ref, kv_ref, ck_ref, cv_ref, cos_ref, sn_ref, sp_ref, o_ref,
                     ckat_ref, cvat_ref):
    n = pl.program_id(1)
    scale = HEAD_DIM ** -0.5

    @pl.when(n == 0)
    def _():
        for c in range(KV_W // LANES):
            for hf in range(2):
                ks = _split_heads(ck_ref[:, c * LANES:(c + 1) * LANES], hf)
                vs = _split_heads(cv_ref[:, c * LANES:(c + 1) * LANES], hf)
                for half in range(2):
                    ckat_ref[c, hf, half] = ks[half]
                    cvat_ref[c, hf, half] = vs[half]

    q0 = pl.multiple_of(n * LAT_BLOCK, LAT_BLOCK)
    k0 = pl.multiple_of(jnp.clip(q0 - LAT_BLOCK, 0, DEC_SEQ - LAT_KEYS), LAT_BLOCK)
    q_tab = [r[pl.ds(q0, LAT_BLOCK), :] for r in (cos_ref, sn_ref, sp_ref)]
    k_tab = [r[pl.ds(k0, LAT_KEYS), :] for r in (cos_ref, sn_ref, sp_ref)]
    qi = lax.broadcasted_iota(jnp.int32, (2 * LAT_BLOCK, LAT_KEYS), 0) & (LAT_BLOCK - 1)
    kj = lax.broadcasted_iota(jnp.int32, (2 * LAT_BLOCK, LAT_KEYS), 1)
    valid = jnp.abs((k0 + kj) - (q0 + qi)) <= WINDOW
    top = lax.broadcasted_iota(jnp.int32, (2 * LAT_BLOCK, 1), 0) < LAT_BLOCK
    for c in range(KV_W // LANES):
        kc = _rope(kv_ref[pl.ds(k0, LAT_KEYS), c * LANES:(c + 1) * LANES], *k_tab)
        vc = kv_ref[pl.ds(k0, LAT_KEYS), KV_W + c * LANES:KV_W + (c + 1) * LANES]
        for hf in range(2):
            kh = 2 * c + hf
            k_at = _split_heads(kc, hf)
            v_at = _split_heads(vc, hf)
            ck_at = [ckat_ref[c, hf, half] for half in range(2)]
            cv_at = [cvat_ref[c, hf, half] for half in range(2)]
            ja, jb = 2 * kh, 2 * kh + 1
            qa = _rope(q_ref[:, ja * LANES:(ja + 1) * LANES], *q_tab)
            qb = _rope(q_ref[:, jb * LANES:(jb + 1) * LANES], *q_tab)
            q2 = (jnp.concatenate([qa, qb], axis=0) * scale).astype(BF16)
            acc = None
            for half in range(2):
                sk = jnp.where(top, sink_ref[2 * ja + half], sink_ref[2 * jb + half])
                s_loc = jnp.where(valid, _dot_nt(q2, k_at[half]), NEG)
                s_ctx = _dot_nt(q2, ck_at[half])
                m = jnp.maximum(jnp.maximum(jnp.max(s_loc, axis=-1, keepdims=True),
                                            jnp.max(s_ctx, axis=-1, keepdims=True)), sk)
                p_loc = jnp.exp(s_loc - m)
                p_ctx = jnp.exp(s_ctx - m)
                den = (jnp.sum(p_loc, axis=-1, keepdims=True)
                       + jnp.sum(p_ctx, axis=-1, keepdims=True) + jnp.exp(sk - m))
                o = (jnp.dot(p_loc.astype(BF16), v_at[half], preferred_element_type=F32)
                     + jnp.dot(p_ctx.astype(BF16), cv_at[half], preferred_element_type=F32)) / den
                acc = o if acc is None else acc + o
            o_ref[:, ja * LANES:(ja + 1) * LANES] = acc[:LAT_BLOCK].astype(BF16)
            o_ref[:, jb * LANES:(jb + 1) * LANES] = acc[LAT_BLOCK:].astype(BF16)


def _lat_attention(z, sink, cache_k, cache_v, layer, tables):
    blocks = DEC_SEQ // LAT_BLOCK
    q_row0 = N_PROMPT // LAT_BLOCK
    kv_row0 = N_PROMPT // DEC_SEQ
    tab_spec = pl.BlockSpec((DEC_SEQ, LANES), lambda b, n: (0, 0))
    cache_spec = pl.BlockSpec((None, None, PAST_LEN, KV_W), lambda b, n: (b, layer, 0, 0))
    return pl.pallas_call(
        _lat_attn_kernel,
        grid=(DEC_BATCH, blocks),
        in_specs=[
            pl.BlockSpec(memory_space=pltpu.SMEM),
            pl.BlockSpec((LAT_BLOCK, ATTN_W), lambda b, n: (q_row0 + b * blocks + n, 0)),
            pl.BlockSpec((DEC_SEQ, 2 * KV_W), lambda b, n: (kv_row0 + b, ATTN_W // (2 * KV_W))),
            cache_spec, cache_spec, tab_spec, tab_spec, tab_spec,
        ],
        out_specs=pl.BlockSpec((LAT_BLOCK, ATTN_W), lambda b, n: (b * blocks + n, 0)),
        out_shape=jax.ShapeDtypeStruct((N_SAMPLE, ATTN_W), BF16),
        scratch_shapes=[pltpu.VMEM((KV_W // LANES, 2, 2, PAST_LEN, LANES), BF16)] * 2,
        compiler_params=_params(("arbitrary", "arbitrary")),
        name="lat_attn",
    )(sink, z, z, cache_k, cache_v, *tables)


def _mix_tile(cur, prev, nxt, prev_ok, next_ok, t0, seq_len, weights, scratch):
    wpool_ref, pscale_ref, dw_ref, cb_ref, lng_ref, lnb_ref, wpw_ref = weights
    pu_ref, hx_ref, rot_ref, cv_ref = scratch
    mid = slice(HALO, HALO + MIX_TILE)
    end = slice(HALO + MIX_TILE, 2 * HALO + MIX_TILE)

    def pu(src):
        ref, rows = src
        return ref[rows, 0:POOL_W]

    def glu(src):
        ref, rows = src
        return ref[rows, POOL_W:POOL_W + CONV_W] * jax.nn.sigmoid(ref[rows, POOL_W + CONV_W:])

    pu_ref[0:HALO, :] = jnp.where(prev_ok, pu(prev), 0.0)
    pu_ref[mid, :] = pu(cur)
    pu_ref[end, :] = jnp.where(next_ok, pu(nxt), 0.0)
    hx_ref[0:HALO, :] = jnp.where(prev_ok, glu(prev), 0.0)
    hx_ref[mid, :] = glu(cur)
    hx_ref[end, :] = jnp.where(next_ok, glu(nxt), 0.0)

    ts = t0 + lax.broadcasted_iota(jnp.int32, (MIX_TILE, 1), 0)
    pool = []
    for gi, w in enumerate(POOL_WINDOWS):
        lanes = slice(gi * POOL_GW, (gi + 1) * POOL_GW)
        wsum = None
        for k in range(w):
            v = pu_ref[pl.ds(HALO - w // 2 + k, MIX_TILE), lanes]
            wsum = v if wsum is None else wsum + v
        lo = jnp.clip(ts - w // 2, 0, seq_len)
        hi = jnp.clip(ts - w // 2 + w, 0, seq_len)
        pooled = wsum / (hi - lo).astype(F32) - pu_ref[mid, lanes]
        y = jnp.dot(pooled.astype(BF16), wpool_ref[gi], preferred_element_type=F32)
        pool.append((y * pscale_ref[:, lanes]).astype(BF16))

    ext_rows = MIX_TILE + 2 * HALO
    hx = hx_ref[...]
    for r in range(1, SUBLANES):
        rot_ref[r - 1] = pltpu.roll(hx, ext_rows - r, 0)[:ROT_ROWS]
    for c in range(0, CONV_W, LANES):
        lanes = slice(c, c + LANES)
        for rb in range(0, MIX_TILE, CONV_ROWS):
            acc = None
            for k in range(CONV_K):
                a, r = divmod(HALO - CONV_K // 2 + k, SUBLANES)
                src = hx_ref if r == 0 else rot_ref.at[r - 1]
                v = src[pl.ds(SUBLANES * a + rb, CONV_ROWS), lanes] * dw_ref[k:k + 1, lanes]
                acc = v if acc is None else acc + v
            cv_ref[rb:rb + CONV_ROWS, lanes] = acc
    h = cv_ref[...] + cb_ref[...]
    mu = jnp.mean(h, axis=-1, keepdims=True)
    var = jnp.mean(jnp.square(h - mu), axis=-1, keepdims=True)
    y = (h - mu) * lax.rsqrt(var + EPS) * lng_ref[...] + lnb_ref[...]
    y = y * jax.nn.sigmoid(y)
    conv = jnp.dot(y.astype(BF16), wpw_ref[...], preferred_element_type=F32).astype(BF16)
    return jnp.concatenate(pool, axis=-1), conv


def _mixout_kernel(*refs, tm, split_x):
    zc_ref, zp_ref, zn_ref, actx_ref, alat_ref = refs[:5]
    x_refs = refs[5:-16]
    mod_ref, g_ref, wo_ref = refs[-16:-13]
    weights = refs[-13:-6]
    o_ref = refs[-6]
    scratch = refs[-5:-1]
    pc_ref = refs[-1]
    i = pl.program_id(0)
    n_tiles = N_TOK // tm
    prompt_tiles = N_PROMPT // tm
    tiles_per_seq = DEC_SEQ // tm
    n_sub = tm // MIX_TILE
    o1 = ATTN_W
    o2 = ATTN_W + POOL_W

    @pl.when(i == 0)
    def _():
        pc_ref[...] = jnp.zeros_like(pc_ref)

    out_is_prompt = jnp.maximum(i - 1, 0) < prompt_tiles
    for s in range(n_sub):
        rows = slice(s * MIX_TILE, (s + 1) * MIX_TILE)
        attn = jnp.where(out_is_prompt, actx_ref[rows, :], alat_ref[rows, :])
        mix = jnp.concatenate([attn, pc_ref[rows, :]], axis=-1)
        acc = jnp.dot(mix, wo_ref[...], preferred_element_type=F32)
        if split_x:
            x = jnp.where(out_is_prompt, x_refs[0][rows, :], x_refs[1][rows, :])
        else:
            x = x_refs[0][rows, :]
        o_ref[rows, :] = x + mod_ref[2:3, :] * _rms(acc, g_ref[...])

    t = jnp.minimum(i, n_tiles - 1)
    is_prompt = t < prompt_tiles
    is_latent = jnp.logical_not(is_prompt)
    j = jnp.maximum(t - prompt_tiles, 0) % tiles_per_seq
    seq_len = jnp.where(is_prompt, SEQ, DEC_SEQ)
    for s in range(n_sub):
        rows = slice(s * MIX_TILE, (s + 1) * MIX_TILE)
        if s == 0:
            prev, prev_ok = (zp_ref, slice(0, HALO)), jnp.logical_and(is_latent, j > 0)
        else:
            prev, prev_ok = (zc_ref, slice(s * MIX_TILE - HALO, s * MIX_TILE)), is_latent
        if s == n_sub - 1:
            nxt, next_ok = (zn_ref, slice(0, HALO)), jnp.logical_and(is_latent, j < tiles_per_seq - 1)
        else:
            nxt, next_ok = (zc_ref, slice((s + 1) * MIX_TILE, (s + 1) * MIX_TILE + HALO)), is_latent
        t0 = jnp.where(is_prompt, 0, j * tm + s * MIX_TILE)
        pool, conv = _mix_tile((zc_ref, rows), prev, nxt, prev_ok, next_ok, t0, seq_len, weights,
                               scratch)
        pc_ref[rows, 0:POOL_W] = pool
        pc_ref[rows, POOL_W:] = conv


def _mix_out(z, attn_ctx, attn_lat, x, mods, g_post, w_out, layer, w_pool, pool_scale, conv_dw,
             conv_b, ln_g, ln_b, w_pw, tm=512):
    split_x = isinstance(x, tuple)
    n_tiles = N_TOK // tm
    halo_per_tile = tm // HALO
    last_halo = N_TOK // HALO - 1
    mix_w = POOL_W + 2 * CONV_W
    col = (IN_W - mix_w) // mix_w
    full = lambda shape: pl.BlockSpec(shape, lambda i: (0,) * len(shape))
    mix_tile = lambda i: jnp.minimum(i, n_tiles - 1)
    out_row = lambda i: (jnp.maximum(i - 1, 0), 0)
    x_specs = (_pair_specs(tm, D_MODEL, lag=1) if split_x else [pl.BlockSpec((tm, D_MODEL), out_row)])
    x_args = list(x) if split_x else [x]
    return pl.pallas_call(
        functools.partial(_mixout_kernel, tm=tm, split_x=split_x),
        grid=(n_tiles + 1,),
        in_specs=[
            pl.BlockSpec((tm, mix_w), lambda i: (mix_tile(i), col)),
            pl.BlockSpec((HALO, mix_w), lambda i: (jnp.maximum(mix_tile(i) * halo_per_tile - 1, 0), col)),
            pl.BlockSpec((HALO, mix_w),
                         lambda i: (jnp.minimum((mix_tile(i) + 1) * halo_per_tile, last_halo), col)),
        ] + _pair_specs(tm, ATTN_W, lag=1) + x_specs + [
            pl.BlockSpec((None, 6, D_MODEL), lambda i: (_cond_group(jnp.maximum(i - 1, 0), tm), 0, 0)),
            pl.BlockSpec((1, D_MODEL), lambda i: (0, 0)),
            pl.BlockSpec((None, D_MODEL, D_MODEL), lambda i: (layer, 0, 0), pipeline_mode=pl.Buffered(1)),
            full((len(POOL_WINDOWS), POOL_GW, POOL_GW)),
            full((1, POOL_W)),
            full((CONV_K, CONV_W)),
            full((1, CONV_W)), full((1, CONV_W)), full((1, CONV_W)),
            full((CONV_W, CONV_W)),
        ],
        out_specs=pl.BlockSpec((tm, D_MODEL), out_row),
        out_shape=jax.ShapeDtypeStruct((N_TOK, D_MODEL), F32),
        scratch_shapes=[pltpu.VMEM((MIX_TILE + 2 * HALO, POOL_W), F32),
                        pltpu.VMEM((MIX_TILE + 2 * HALO, CONV_W), F32),
                        pltpu.VMEM((SUBLANES - 1, ROT_ROWS, CONV_W), F32),
                        pltpu.VMEM((MIX_TILE, CONV_W), F32),
                        pltpu.VMEM((tm, POOL_W + CONV_W), BF16)],
        compiler_params=_params(("arbitrary",)),
        name="mix_out",
    )(z, z, z, attn_ctx, attn_lat, *x_args, mods, g_post, w_out, w_pool, pool_scale, conv_dw,
      conv_b, ln_g, ln_b, w_pw)


def _mlp_kernel(*refs, tm, sub, split_out):
    x_ref, mod_ref, gpre_ref, gpost_ref, w1_ref, w2_ref = refs[:6]
    if split_out:
        yp_ref, ys_ref, h_ref, acc_ref = refs[6:]
    else:
        o_ref, h_ref = refs[6:]
        acc_ref = o_ref
    i = pl.program_id(0)
    j = pl.program_id(1)
    last = pl.num_programs(1) - 1

    def step(first, out_ref):
        for r in range(0, tm, sub):
            rows = slice(r, r + sub)
            if first:
                y = _rms(x_ref[rows, :], gpre_ref[...])
                h = (y * (1.0 + mod_ref[4:5, :]) + mod_ref[3:4, :]).astype(BF16)
                h_ref[rows, :] = h
            else:
                h = h_ref[rows, :]
            a = jnp.dot(h, w1_ref[...], preferred_element_type=F32)
            a = jnp.square(jnp.maximum(a, 0.0)).astype(BF16)
            acc = jnp.dot(a, w2_ref[...], preferred_element_type=F32)
            if not first:
                acc += acc_ref[rows, :]
            if out_ref is None:
                acc_ref[rows, :] = acc
            else:
                out_ref[rows, :] = x_ref[rows, :] + mod_ref[5:6, :] * _rms(acc, gpost_ref[...])

    pl.when(j == 0)(lambda: step(True, None))
    pl.when(jnp.logical_and(j > 0, j < last))(lambda: step(False, None))
    if split_out:
        is_prompt = i < N_PROMPT // tm
        pl.when(jnp.logical_and(j == last, is_prompt))(lambda: step(False, yp_ref))
        pl.when(jnp.logical_and(j == last, jnp.logical_not(is_prompt)))(lambda: step(False, ys_ref))
    else:
        pl.when(j == last)(lambda: step(False, o_ref))


def _mlp(x, mods, g_pre, g_post, w1, w2, layer, split_out=False, tm=1024, tf=1024, sub=512):
    if split_out:
        out_specs = _pair_specs(tm, D_MODEL, grid_rank=2)
        out_shape = [jax.ShapeDtypeStruct((N_PROMPT, D_MODEL), F32),
                     jax.ShapeDtypeStruct((N_SAMPLE, D_MODEL), F32)]
        scratch = [pltpu.VMEM((tm, D_MODEL), BF16), pltpu.VMEM((tm, D_MODEL), F32)]
    else:
        out_specs = pl.BlockSpec((tm, D_MODEL), lambda i, j: (i, 0))
        out_shape = jax.ShapeDtypeStruct((N_TOK, D_MODEL), F32)
        scratch = [pltpu.VMEM((tm, D_MODEL), BF16)]
    return pl.pallas_call(
        functools.partial(_mlp_kernel, tm=tm, sub=min(sub, tm), split_out=split_out),
        grid=(N_TOK // tm, D_FF // tf),
        in_specs=[
            pl.BlockSpec((tm, D_MODEL), lambda i, j: (i, 0)),
            pl.BlockSpec((None, 6, D_MODEL), lambda i, j: (_cond_group(i, tm), 0, 0)),
            pl.BlockSpec((1, D_MODEL), lambda i, j: (0, 0)),
            pl.BlockSpec((1, D_MODEL), lambda i, j: (0, 0)),
            pl.BlockSpec((None, D_MODEL, tf), lambda i, j: (layer, 0, j)),
            pl.BlockSpec((None, tf, D_MODEL), lambda i, j: (layer, j, 0)),
        ],
        out_specs=out_specs,
        out_shape=out_shape,
        scratch_shapes=scratch,
        compiler_params=_params(("arbitrary", "arbitrary")),
        name="mlp",
    )(x, mods, g_pre, g_post, w1, w2)


def kernel(x_prompt, x_sample, cache_k, cache_v, c, c_ctx, w_ada, b_ada, g_pre1, g_post1, g_pre2,
           g_post2, w_in, sink, w_pool, pool_scale, conv_dw, conv_b, conv_ln_g, conv_ln_b,
           w_conv_pw, w_out, w_mlp1, w_mlp2):
    cond = jnp.concatenate([c_ctx[None, :], c, jnp.zeros((N_COND - 1 - DEC_BATCH, D_MODEL), F32)], axis=0)
    mods = _adaln(cond, w_ada, b_ada).reshape(DEPTH, N_COND, 6, D_MODEL)
    x = (x_prompt.reshape(N_PROMPT, D_MODEL), x_sample.reshape(N_SAMPLE, D_MODEL))
    ck = cache_k.reshape(DEC_BATCH, DEPTH, PAST_LEN, KV_W)
    cv = cache_v.reshape(DEC_BATCH, DEPTH, PAST_LEN, KV_W)
    tables = _rope_tables()
    w_in_b, w_out_b = w_in.astype(BF16), w_out.astype(BF16)
    w1_b, w2_b = w_mlp1.astype(BF16), w_mlp2.astype(BF16)
    w_pool_b, w_pw_b = w_pool.astype(BF16), w_conv_pw.astype(BF16)
    vec = lambda a, l: a[l][None, :]
    kv_bufs = (jnp.zeros((BATCH, DEPTH, SEQ, KV_W), F32), jnp.zeros((BATCH, DEPTH, SEQ, KV_W), F32))
    for l in range(DEPTH):
        z, new_k, new_v = _in_proj(x, mods[l], vec(g_pre1, l), w_in_b, l, kv_bufs)
        kv_bufs = (new_k, new_v)
        attn_ctx = _ctx_attention(z, sink[l])
        attn_lat = _lat_attention(z, sink[l], ck, cv, l, tables)
        x = _mix_out(z, attn_ctx, attn_lat, x, mods[l], vec(g_post1, l), w_out_b, l, w_pool_b[l],
                     vec(pool_scale, l), conv_dw[l], vec(conv_b, l), vec(conv_ln_g, l),
                     vec(conv_ln_b, l), w_pw_b[l])
        last = l == DEPTH - 1
        x = _mlp(x, mods[l], vec(g_pre2, l), vec(g_post2, l), w1_b, w2_b, l, split_out=last,
                 tm=512 if last else 1024)
    kv_shape = (BATCH, DEPTH, SEQ, N_KV_HEADS, HEAD_DIM)
    return (x[0].reshape(BATCH, SEQ, D_MODEL), x[1].reshape(DEC_BATCH, DEC_SEQ, D_MODEL),
            new_k.reshape(kv_shape), new_v.reshape(kv_shape))
```

```python
import functools

import numpy as np
import jax
import jax.numpy as jnp
from jax import lax
from jax.experimental import pallas as pl
from jax.experimental.pallas import tpu as pltpu

D_MODEL = 2048
BATCH = 32
SEQ = 256
DEPTH = 4
DEC_BATCH = 2
DEC_SEQ = 2048
PAST_LEN = 512
GRID_W = 64
HEAD_DIM = 64
ATTN_W = 1024
N_HEADS = 16
N_KV_HEADS = 4
KV_W = 256
POOL_W = 512
POOL_WINDOWS = (2, 4, 8, 16)
POOL_GW = 128
CONV_W = 512
CONV_K = 31
IN_W = 3072
WINDOW = 128
D_FF = 4 * D_MODEL
ROPE_BASE = 10000.0
EPS = 1e-6
NEG = -1e30
LOG2E = 1.4426950408889634

N_PROMPT = BATCH * SEQ
N_SAMPLE = DEC_BATCH * DEC_SEQ
N_TOK = N_PROMPT + N_SAMPLE
N_COND = 8
LANES = 128
HALO = 16
MIX_TILE = 256
SUBLANES = 8
ROT_ROWS = MIX_TILE + HALO + SUBLANES
CONV_ROWS = 64
LAT_BLOCK = 128
LAT_KEYS = 3 * LAT_BLOCK

F32 = jnp.float32
BF16 = jnp.bfloat16

_VMEM_LIMIT = 60 * 1024 * 1024


def _params(sem):
    return pltpu.CompilerParams(dimension_semantics=sem, vmem_limit_bytes=_VMEM_LIMIT)


def _cond_group(i, tm):
    per_seq = DEC_SEQ // tm
    return jnp.maximum(i - N_PROMPT // tm + per_seq, 0) // per_seq


def _rms(x, g):
    return x * lax.rsqrt(jnp.mean(x * x, axis=-1, keepdims=True) + EPS) * g


def _ada_kernel(c_ref, w_ref, b_ref, o_ref):
    c = c_ref[...]
    s = (c * jax.nn.sigmoid(c)).astype(BF16)
    o_ref[...] = jnp.dot(s, w_ref[...].astype(BF16), preferred_element_type=F32) + b_ref[...]


def _adaln(cond, w_ada, b_ada, tn=1024):
    n_out = 6 * D_MODEL
    return pl.pallas_call(
        _ada_kernel,
        grid=(DEPTH, n_out // tn),
        in_specs=[
            pl.BlockSpec((N_COND, D_MODEL), lambda l, j: (0, 0)),
            pl.BlockSpec((None, D_MODEL, tn), lambda l, j: (l, 0, j)),
            pl.BlockSpec((None, 1, tn), lambda l, j: (l, 0, j)),
        ],
        out_specs=pl.BlockSpec((None, N_COND, tn), lambda l, j: (l, 0, j)),
        out_shape=jax.ShapeDtypeStruct((DEPTH, N_COND, n_out), F32),
        compiler_params=_params(("parallel", "parallel")),
        name="adaln",
    )(cond, w_ada, b_ada.reshape(DEPTH, 1, n_out))


def _pair_specs(tm, width, grid_rank=1, lag=0):
    n = N_PROMPT // tm
    m = N_SAMPLE // tm
    prompt = lambda i: (jnp.clip(i - lag, 0, n - 1), 0)
    latent = lambda i: (jnp.clip(i - lag - n, 0, m - 1), 0)
    if grid_rank == 1:
        return [pl.BlockSpec((tm, width), prompt), pl.BlockSpec((tm, width), latent)]
    return [pl.BlockSpec((tm, width), lambda i, j: prompt(i)),
            pl.BlockSpec((tm, width), lambda i, j: latent(i))]


def _inproj_kernel(*refs, tm, tn, sub, split_x):
    nx = 2 if split_x else 1
    x_refs = refs[:nx]
    mod_ref, g_ref, w_ref = refs[nx:nx + 3]
    z_ref, k_ref, v_ref, h_ref = refs[-4:]
    i = pl.program_id(0)
    is_prompt = i < N_PROMPT // tm
    for r in range(0, tm, sub):
        rows = slice(r, r + sub)
        if split_x:
            x = jnp.where(is_prompt, x_refs[0][rows, :], x_refs[1][rows, :])
        else:
            x = x_refs[0][rows, :]
        y = _rms(x, g_ref[...])
        h_ref[rows, :] = (y * (1.0 + mod_ref[1:2, :]) + mod_ref[0:1, :]).astype(BF16)
        for n in range(0, IN_W, tn):
            z_ref[rows, n:n + tn] = jnp.dot(h_ref[rows, :], w_ref[:, n:n + tn],
                                            preferred_element_type=F32)

    @pl.when(is_prompt)
    def _():
        for s in range(tm // SEQ):
            rows = slice(s * SEQ, (s + 1) * SEQ)
            k_ref[s] = z_ref[rows, ATTN_W:ATTN_W + KV_W]
            v_ref[s] = z_ref[rows, ATTN_W + KV_W:ATTN_W + 2 * KV_W]


def _in_proj(x, mods, g_pre, w_in, layer, kv_bufs, tm=512, tn=512, sub=512):
    split_x = isinstance(x, tuple)
    n = N_PROMPT // tm
    x_specs = _pair_specs(tm, D_MODEL) if split_x else [pl.BlockSpec((tm, D_MODEL), lambda i: (i, 0))]
    x_args = list(x) if split_x else [x]
    kv_spec = pl.BlockSpec((tm // SEQ, None, SEQ, KV_W), lambda i: (jnp.minimum(i, n - 1), layer, 0, 0))
    kv_shape = jax.ShapeDtypeStruct((BATCH, DEPTH, SEQ, KV_W), F32)
    n_in = len(x_args) + 3
    alias_specs = [pl.BlockSpec(memory_space=pl.ANY)] * 2
    alias_args = list(kv_bufs)
    aliases = {n_in: 1, n_in + 1: 2}
    return pl.pallas_call(
        functools.partial(_inproj_kernel, tm=tm, tn=tn, sub=sub, split_x=split_x),
        grid=(N_TOK // tm,),
        in_specs=x_specs + [
            pl.BlockSpec((None, 6, D_MODEL), lambda i: (_cond_group(i, tm), 0, 0)),
            pl.BlockSpec((1, D_MODEL), lambda i: (0, 0)),
            pl.BlockSpec((D_MODEL, IN_W), lambda i: (0, 0), pipeline_mode=pl.Buffered(1)),
        ] + alias_specs,
        out_specs=[pl.BlockSpec((tm, IN_W), lambda i: (i, 0)), kv_spec, kv_spec],
        out_shape=[jax.ShapeDtypeStruct((N_TOK, IN_W), F32), kv_shape, kv_shape],
        scratch_shapes=[pltpu.VMEM((tm, D_MODEL), BF16)],
        input_output_aliases=aliases,
        compiler_params=_params(("arbitrary",)),
        name="in_proj",
    )(*x_args, mods, g_pre, w_in, *alias_args)


def _split_heads(x, half):
    lane = lax.broadcasted_iota(jnp.int32, x.shape, 1)
    own = jnp.where((lane < HEAD_DIM) == (half == 0), x, 0.0)
    swapped = pltpu.roll(own, HEAD_DIM, 1)
    pair = [own, swapped] if half == 0 else [swapped, own]
    return [p.astype(BF16) for p in pair]


def _dot_nt(a, b):
    return lax.dot_general(a, b, (((1,), (1,)), ((), ())), preferred_element_type=F32)


def _cast_stream(weights, steps, step_index):
    in_specs, out_specs, out_shapes = [], [], []
    for w, layer in weights:
        _, rows, cols = w.shape
        slab = rows // steps
        in_specs.append(pl.BlockSpec((None, slab, cols),
                                     lambda *g, layer=layer: (layer, step_index(*g), 0)))
        out_specs.append(pl.BlockSpec((slab, cols), lambda *g: (step_index(*g), 0)))
        out_shapes.append(jax.ShapeDtypeStruct((rows, cols), BF16))
    return in_specs, out_specs, out_shapes


def _cast_slabs(src_refs, dst_refs):
    for src, dst in zip(src_refs, dst_refs):
        dst[...] = src[...].astype(BF16)


def _ctx_attn_kernel(*refs, n_cast):
    sink_ref, q_ref, kv_ref = refs[:3]
    o_ref = refs[3 + n_cast]
    _cast_slabs(refs[3:3 + n_cast], refs[4 + n_cast:])
    scale = HEAD_DIM ** -0.5 * LOG2E
    for c in range(KV_W // LANES):
        kc = kv_ref[:, c * LANES:(c + 1) * LANES]
        vc = kv_ref[:, KV_W + c * LANES:KV_W + (c + 1) * LANES]
        for hf in range(2):
            kh = 2 * c + hf
            k_at = _split_heads(kc, hf)
            v_at = _split_heads(vc, hf)
            for qc in range(2):
                j = 2 * kh + qc
                q = (q_ref[:, j * LANES:(j + 1) * LANES] * scale).astype(BF16)
                acc = None
                for half in range(2):
                    sk = sink_ref[2 * j + half] * LOG2E
                    s = _dot_nt(q, k_at[half])
                    m = jnp.maximum(jnp.max(s, axis=-1, keepdims=True), sk)
                    p = jnp.exp2(s - m)
                    den = jnp.sum(p, axis=-1, keepdims=True) + jnp.exp2(sk - m)
                    o = jnp.dot(p.astype(BF16), v_at[half], preferred_element_type=F32) / den
                    acc = o if acc is None else acc + o
                o_ref[:, j * LANES:(j + 1) * LANES] = acc.astype(BF16)


def _ctx_attention(z, sink, cast):
    cast_in, cast_out, cast_shapes = _cast_stream(cast, BATCH, lambda b: b)
    return pl.pallas_call(
        functools.partial(_ctx_attn_kernel, n_cast=len(cast)),
        grid=(BATCH,),
        in_specs=[
            pl.BlockSpec(memory_space=pltpu.SMEM),
            pl.BlockSpec((SEQ, ATTN_W), lambda b: (b, 0)),
            pl.BlockSpec((SEQ, 2 * KV_W), lambda b: (b, ATTN_W // (2 * KV_W))),
        ] + cast_in,
        out_specs=[pl.BlockSpec((SEQ, ATTN_W), lambda b: (b, 0))] + cast_out,
        out_shape=[jax.ShapeDtypeStruct((N_PROMPT, ATTN_W), BF16)] + cast_shapes,
        compiler_params=_params(("parallel",)),
        name="ctx_attn",
    )(sink, z, z, *[w for w, _ in cast])


def _rope_tables():
    quarter = HEAD_DIM // 4
    inv_freq = 1.0 / (ROPE_BASE ** (np.arange(quarter, dtype=np.float32) / quarter))
    t = np.arange(DEC_SEQ)
    row_pos = (t // GRID_W).astype(np.float32)
    col_pos = (t % GRID_W).astype(np.float32)
    d = np.arange(LANES) % HEAD_DIM
    pos = np.where(d[None, :] < HEAD_DIM // 2, row_pos[:, None], col_pos[:, None])
    ang = (pos * inv_freq[d % quarter][None, :]).astype(np.float32)
    first = (d % (2 * quarter)) < quarter
    cos = np.cos(ang)
    sin = np.sin(ang)
    sin_next = np.where(first[None, :], -sin, 0.0)
    sin_prev = np.where(first[None, :], 0.0, sin)
    return (jnp.asarray(cos, F32), jnp.asarray(sin_next, F32), jnp.asarray(sin_prev, F32))


def _rope(x, cos, sin_next, sin_prev):
    quarter = HEAD_DIM // 4
    return (x * cos + pltpu.roll(x, LANES - quarter, 1) * sin_next
            + pltpu.roll(x, quarter, 1) * sin_prev)


def _lat_attn_kernel(*refs, n_cast):
    sink_ref, q_ref, kv_ref, ck_ref, cv_ref, cos_ref, sn_ref, sp_ref = refs[:8]
    o_ref = refs[8 + n_cast]
    ckat_ref, cvat_ref = refs[-2:]
    _cast_slabs(refs[8:8 + n_cast], refs[9 + n_cast:-2])
    n = pl.program_id(1)
    scale = HEAD_DIM ** -0.5 * LOG2E

    @pl.when(n == 0)
    def _():
        for c in range(KV_W // LANES):
            for hf in range(2):
                ks = _split_heads(ck_ref[:, c * LANES:(c + 1) * LANES], hf)
                vs = _split_heads(cv_ref[:, c * LANES:(c + 1) * LANES], hf)
                for half in range(2):
                    ckat_ref[c, hf, half] = ks[half]
                    cvat_ref[c, hf, half] = vs[half]

    q0 = pl.multiple_of(n * LAT_BLOCK, LAT_BLOCK)
    k0 = pl.multiple_of(jnp.clip(q0 - LAT_BLOCK, 0, DEC_SEQ - LAT_KEYS), LAT_BLOCK)
    q_tab = [r[pl.ds(q0, LAT_BLOCK), :] for r in (cos_ref, sn_ref, sp_ref)]
    k_tab = [r[pl.ds(k0, LAT_KEYS), :] for r in (cos_ref, sn_ref, sp_ref)]
    qi = lax.broadcasted_iota(jnp.int32, (2 * LAT_BLOCK, LAT_KEYS), 0) & (LAT_BLOCK - 1)
    kj = lax.broadcasted_iota(jnp.int32, (2 * LAT_BLOCK, LAT_KEYS), 1)
    valid = jnp.abs((k0 + kj) - (q0 + qi)) <= WINDOW
    top = lax.broadcasted_iota(jnp.int32, (2 * LAT_BLOCK, 1), 0) < LAT_BLOCK
    for c in range(KV_W // LANES):
        kc = _rope(kv_ref[pl.ds(k0, LAT_KEYS), c * LANES:(c + 1) * LANES], *k_tab)
        vc = kv_ref[pl.ds(k0, LAT_KEYS), KV_W + c * LANES:KV_W + (c + 1) * LANES]
        for hf in range(2):
            kh = 2 * c + hf
            k_at = _split_heads(kc, hf)
            v_at = _split_heads(vc, hf)
            ck_at = [ckat_ref[c, hf, half] for half in range(2)]
            cv_at = [cvat_ref[c, hf, half] for half in range(2)]
            ja, jb = 2 * kh, 2 * kh + 1
            qa = _rope(q_ref[:, ja * LANES:(ja + 1) * LANES], *q_tab)
            qb = _rope(q_ref[:, jb * LANES:(jb + 1) * LANES], *q_tab)
            q2 = (jnp.concatenate([qa, qb], axis=0) * scale).astype(BF16)
            acc = None
            for half in range(2):
                sk = jnp.where(top, sink_ref[2 * ja + half], sink_ref[2 * jb + half]) * LOG2E
                s_loc = jnp.where(valid, _dot_nt(q2, k_at[half]), NEG)
                s_ctx = _dot_nt(q2, ck_at[half])
                m = jnp.maximum(jnp.maximum(jnp.max(s_loc, axis=-1, keepdims=True),
                                            jnp.max(s_ctx, axis=-1, keepdims=True)), sk)
                p_loc = jnp.exp2(s_loc - m)
                p_ctx = jnp.exp2(s_ctx - m)
                den = (jnp.sum(p_loc, axis=-1, keepdims=True)
                       + jnp.sum(p_ctx, axis=-1, keepdims=True) + jnp.exp2(sk - m))
                o = (jnp.dot(p_loc.astype(BF16), v_at[half], preferred_element_type=F32)
                     + jnp.dot(p_ctx.astype(BF16), cv_at[half], preferred_element_type=F32)) / den
                acc = o if acc is None else acc + o
            o_ref[:, ja * LANES:(ja + 1) * LANES] = acc[:LAT_BLOCK].astype(BF16)
            o_ref[:, jb * LANES:(jb + 1) * LANES] = acc[LAT_BLOCK:].astype(BF16)


def _lat_attention(z, sink, cache_k, cache_v, layer, tables, cast):
    blocks = DEC_SEQ // LAT_BLOCK
    cast_in, cast_out, cast_shapes = _cast_stream(cast, DEC_BATCH * blocks, lambda b, n: b * blocks + n)
    q_row0 = N_PROMPT // LAT_BLOCK
    kv_row0 = N_PROMPT // DEC_SEQ
    tab_spec = pl.BlockSpec((DEC_SEQ, LANES), lambda b, n: (0, 0))
    cache_spec = pl.BlockSpec((None, None, PAST_LEN, KV_W), lambda b, n: (b, layer, 0, 0))
    return pl.pallas_call(
        functools.partial(_lat_attn_kernel, n_cast=len(cast)),
        grid=(DEC_BATCH, blocks),
        in_specs=[
            pl.BlockSpec(memory_space=pltpu.SMEM),
            pl.BlockSpec((LAT_BLOCK, ATTN_W), lambda b, n: (q_row0 + b * blocks + n, 0)),
            pl.BlockSpec((DEC_SEQ, 2 * KV_W), lambda b, n: (kv_row0 + b, ATTN_W // (2 * KV_W))),
            cache_spec, cache_spec, tab_spec, tab_spec, tab_spec,
        ] + cast_in,
        out_specs=[pl.BlockSpec((LAT_BLOCK, ATTN_W), lambda b, n: (b * blocks + n, 0))] + cast_out,
        out_shape=[jax.ShapeDtypeStruct((N_SAMPLE, ATTN_W), BF16)] + cast_shapes,
        scratch_shapes=[pltpu.VMEM((KV_W // LANES, 2, 2, PAST_LEN, LANES), BF16)] * 2,
        compiler_params=_params(("arbitrary", "arbitrary")),
        name="lat_attn",
    )(sink, z, z, cache_k, cache_v, *tables, *[w for w, _ in cast])


def _mix_tile(cur, prev, nxt, prev_ok, next_ok, t0, seq_len, weights, scratch):
    wpool_ref, pscale_ref, dw_ref, cb_ref, lng_ref, lnb_ref, wpw_ref = weights
    pu_ref, hx_ref, rot_ref, cv_ref = scratch
    mid = slice(HALO, HALO + MIX_TILE)
    end = slice(HALO + MIX_TILE, 2 * HALO + MIX_TILE)

    def pu(src):
        ref, rows = src
        return ref[rows, 0:POOL_W]

    def glu(src):
        ref, rows = src
        return ref[rows, POOL_W:POOL_W + CONV_W] * jax.nn.sigmoid(ref[rows, POOL_W + CONV_W:])

    pu_ref[0:HALO, :] = jnp.where(prev_ok, pu(prev), 0.0)
    pu_ref[mid, :] = pu(cur)
    pu_ref[end, :] = jnp.where(next_ok, pu(nxt), 0.0)
    hx_ref[0:HALO, :] = jnp.where(prev_ok, glu(prev), 0.0)
    hx_ref[mid, :] = glu(cur)
    hx_ref[end, :] = jnp.where(next_ok, glu(nxt), 0.0)

    ts = t0 + lax.broadcasted_iota(jnp.int32, (MIX_TILE, 1), 0)
    pool = []
    for gi, w in enumerate(POOL_WINDOWS):
        lanes = slice(gi * POOL_GW, (gi + 1) * POOL_GW)
        wsum = None
        for k in range(w):
            v = pu_ref[pl.ds(HALO - w // 2 + k, MIX_TILE), lanes]
            wsum = v if wsum is None else wsum + v
        lo = jnp.clip(ts - w // 2, 0, seq_len)
        hi = jnp.clip(ts - w // 2 + w, 0, seq_len)
        pooled = wsum / (hi - lo).astype(F32) - pu_ref[mid, lanes]
        y = jnp.dot(pooled.astype(BF16), wpool_ref[gi], preferred_element_type=F32)
        pool.append((y * pscale_ref[:, lanes]).astype(BF16))

    ext_rows = MIX_TILE + 2 * HALO
    hx = hx_ref[...]
    for r in range(1, SUBLANES):
        rot_ref[r - 1] = pltpu.roll(hx, ext_rows - r, 0)[:ROT_ROWS]
    for c in range(0, CONV_W, LANES):
        lanes = slice(c, c + LANES)
        for rb in range(0, MIX_TILE, CONV_ROWS):
            acc = None
            for k in range(CONV_K):
                a, r = divmod(HALO - CONV_K // 2 + k, SUBLANES)
                src = hx_ref if r == 0 else rot_ref.at[r - 1]
                v = src[pl.ds(SUBLANES * a + rb, CONV_ROWS), lanes] * dw_ref[k:k + 1, lanes]
                acc = v if acc is None else acc + v
            cv_ref[rb:rb + CONV_ROWS, lanes] = acc
    h = cv_ref[...] + cb_ref[...]
    mu = jnp.mean(h, axis=-1, keepdims=True)
    var = jnp.mean(jnp.square(h - mu), axis=-1, keepdims=True)
    y = (h - mu) * lax.rsqrt(var + EPS) * lng_ref[...] + lnb_ref[...]
    y = y * jax.nn.sigmoid(y)
    conv = jnp.dot(y.astype(BF16), wpw_ref[...], preferred_element_type=F32).astype(BF16)
    return jnp.concatenate(pool, axis=-1), conv


def _mixout_kernel(*refs, tm, split_x):
    zc_ref, zp_ref, zn_ref, actx_ref, alat_ref = refs[:5]
    x_refs = refs[5:-16]
    mod_ref, g_ref, wo_ref = refs[-16:-13]
    weights = refs[-13:-6]
    o_ref = refs[-6]
    scratch = refs[-5:-1]
    pc_ref = refs[-1]
    i = pl.program_id(0)
    n_tiles = N_TOK // tm
    prompt_tiles = N_PROMPT // tm
    tiles_per_seq = DEC_SEQ // tm
    n_sub = tm // MIX_TILE
    o1 = ATTN_W
    o2 = ATTN_W + POOL_W

    @pl.when(i == 0)
    def _():
        pc_ref[...] = jnp.zeros_like(pc_ref)

    out_is_prompt = jnp.maximum(i - 1, 0) < prompt_tiles
    for s in range(n_sub):
        rows = slice(s * MIX_TILE, (s + 1) * MIX_TILE)
        attn = jnp.where(out_is_prompt, actx_ref[rows, :], alat_ref[rows, :])
        mix = jnp.concatenate([attn, pc_ref[rows, :]], axis=-1)
        acc = jnp.dot(mix, wo_ref[...], preferred_element_type=F32)
        if split_x:
            x = jnp.where(out_is_prompt, x_refs[0][rows, :], x_refs[1][rows, :])
        else:
            x = x_refs[0][rows, :]
        o_ref[rows, :] = x + mod_ref[2:3, :] * _rms(acc, g_ref[...])

    t = jnp.minimum(i, n_tiles - 1)
    is_prompt = t < prompt_tiles
    is_latent = jnp.logical_not(is_prompt)
    j = jnp.maximum(t - prompt_tiles, 0) % tiles_per_seq
    seq_len = jnp.where(is_prompt, SEQ, DEC_SEQ)
    for s in range(n_sub):
        rows = slice(s * MIX_TILE, (s + 1) * MIX_TILE)
        if s == 0:
            prev, prev_ok = (zp_ref, slice(0, HALO)), jnp.logical_and(is_latent, j > 0)
        else:
            prev, prev_ok = (zc_ref, slice(s * MIX_TILE - HALO, s * MIX_TILE)), is_latent
        if s == n_sub - 1:
            nxt, next_ok = (zn_ref, slice(0, HALO)), jnp.logical_and(is_latent, j < tiles_per_seq - 1)
        else:
            nxt, next_ok = (zc_ref, slice((s + 1) * MIX_TILE, (s + 1) * MIX_TILE + HALO)), is_latent
        t0 = jnp.where(is_prompt, 0, j * tm + s * MIX_TILE)
        pool, conv = _mix_tile((zc_ref, rows), prev, nxt, prev_ok, next_ok, t0, seq_len, weights,
                               scratch)
        pc_ref[rows, 0:POOL_W] = pool
        pc_ref[rows, POOL_W:] = conv


def _mix_out(z, attn_ctx, attn_lat, x, mods, g_post, w_out, w_pool, pool_scale, conv_dw,
             conv_b, ln_g, ln_b, w_pw, tm=512):
    split_x = isinstance(x, tuple)
    n_tiles = N_TOK // tm
    halo_per_tile = tm // HALO
    last_halo = N_TOK // HALO - 1
    mix_w = POOL_W + 2 * CONV_W
    col = (IN_W - mix_w) // mix_w
    full = lambda shape: pl.BlockSpec(shape, lambda i: (0,) * len(shape))
    mix_tile = lambda i: jnp.minimum(i, n_tiles - 1)
    out_row = lambda i: (jnp.maximum(i - 1, 0), 0)
    x_specs = (_pair_specs(tm, D_MODEL, lag=1) if split_x else [pl.BlockSpec((tm, D_MODEL), out_row)])
    x_args = list(x) if split_x else [x]
    return pl.pallas_call(
        functools.partial(_mixout_kernel, tm=tm, split_x=split_x),
        grid=(n_tiles + 1,),
        in_specs=[
            pl.BlockSpec((tm, mix_w), lambda i: (mix_tile(i), col)),
            pl.BlockSpec((HALO, mix_w), lambda i: (jnp.maximum(mix_tile(i) * halo_per_tile - 1, 0), col)),
            pl.BlockSpec((HALO, mix_w),
                         lambda i: (jnp.minimum((mix_tile(i) + 1) * halo_per_tile, last_halo), col)),
        ] + _pair_specs(tm, ATTN_W, lag=1) + x_specs + [
            pl.BlockSpec((None, 6, D_MODEL), lambda i: (_cond_group(jnp.maximum(i - 1, 0), tm), 0, 0)),
            pl.BlockSpec((1, D_MODEL), lambda i: (0, 0)),
            pl.BlockSpec((D_MODEL, D_MODEL), lambda i: (0, 0), pipeline_mode=pl.Buffered(1)),
            full((len(POOL_WINDOWS), POOL_GW, POOL_GW)),
            full((1, POOL_W)),
            full((CONV_K, CONV_W)),
            full((1, CONV_W)), full((1, CONV_W)), full((1, CONV_W)),
            full((CONV_W, CONV_W)),
        ],
        out_specs=pl.BlockSpec((tm, D_MODEL), out_row),
        out_shape=jax.ShapeDtypeStruct((N_TOK, D_MODEL), F32),
        scratch_shapes=[pltpu.VMEM((MIX_TILE + 2 * HALO, POOL_W), F32),
                        pltpu.VMEM((MIX_TILE + 2 * HALO, CONV_W), F32),
                        pltpu.VMEM((SUBLANES - 1, ROT_ROWS, CONV_W), F32),
                        pltpu.VMEM((MIX_TILE, CONV_W), F32),
                        pltpu.VMEM((tm, POOL_W + CONV_W), BF16)],
        compiler_params=_params(("arbitrary",)),
        name="mix_out",
    )(z, z, z, attn_ctx, attn_lat, *x_args, mods, g_post, w_out, w_pool, pool_scale, conv_dw,
      conv_b, ln_g, ln_b, w_pw)


def _mlp_kernel(*refs, tm, sub, split_out):
    x_ref, mod_ref, gpre_ref, gpost_ref, w1_ref, w2_ref = refs[:6]
    if split_out:
        yp_ref, ys_ref, h_ref, acc_ref = refs[6:]
    else:
        o_ref, h_ref = refs[6:]
        acc_ref = o_ref
    i = pl.program_id(0)
    j = pl.program_id(1)
    last = pl.num_programs(1) - 1

    def step(first, out_ref):
        for r in range(0, tm, sub):
            rows = slice(r, r + sub)
            if first:
                y = _rms(x_ref[rows, :], gpre_ref[...])
                h = (y * (1.0 + mod_ref[4:5, :]) + mod_ref[3:4, :]).astype(BF16)
                h_ref[rows, :] = h
            else:
                h = h_ref[rows, :]
            a = jnp.dot(h, w1_ref[...], preferred_element_type=F32)
            a = jnp.square(jnp.maximum(a, 0.0)).astype(BF16)
            acc = jnp.dot(a, w2_ref[...], preferred_element_type=F32)
            if not first:
                acc += acc_ref[rows, :]
            if out_ref is None:
                acc_ref[rows, :] = acc
            else:
                out_ref[rows, :] = x_ref[rows, :] + mod_ref[5:6, :] * _rms(acc, gpost_ref[...])

    pl.when(j == 0)(lambda: step(True, None))
    pl.when(jnp.logical_and(j > 0, j < last))(lambda: step(False, None))
    if split_out:
        is_prompt = i < N_PROMPT // tm
        pl.when(jnp.logical_and(j == last, is_prompt))(lambda: step(False, yp_ref))
        pl.when(jnp.logical_and(j == last, jnp.logical_not(is_prompt)))(lambda: step(False, ys_ref))
    else:
        pl.when(j == last)(lambda: step(False, o_ref))


def _mlp(x, mods, g_pre, g_post, w1, w2, split_out=False, tm=1024, tf=1024, sub=512):
    if split_out:
        out_specs = _pair_specs(tm, D_MODEL, grid_rank=2)
        out_shape = [jax.ShapeDtypeStruct((N_PROMPT, D_MODEL), F32),
                     jax.ShapeDtypeStruct((N_SAMPLE, D_MODEL), F32)]
        scratch = [pltpu.VMEM((tm, D_MODEL), BF16), pltpu.VMEM((tm, D_MODEL), F32)]
    else:
        out_specs = pl.BlockSpec((tm, D_MODEL), lambda i, j: (i, 0))
        out_shape = jax.ShapeDtypeStruct((N_TOK, D_MODEL), F32)
        scratch = [pltpu.VMEM((tm, D_MODEL), BF16)]
    return pl.pallas_call(
        functools.partial(_mlp_kernel, tm=tm, sub=min(sub, tm), split_out=split_out),
        grid=(N_TOK // tm, D_FF // tf),
        in_specs=[
            pl.BlockSpec((tm, D_MODEL), lambda i, j: (i, 0)),
            pl.BlockSpec((None, 6, D_MODEL), lambda i, j: (_cond_group(i, tm), 0, 0)),
            pl.BlockSpec((1, D_MODEL), lambda i, j: (0, 0)),
            pl.BlockSpec((1, D_MODEL), lambda i, j: (0, 0)),
            pl.BlockSpec((D_MODEL, tf), lambda i, j: (0, j)),
            pl.BlockSpec((tf, D_MODEL), lambda i, j: (j, 0)),
        ],
        out_specs=out_specs,
        out_shape=out_shape,
        scratch_shapes=scratch,
        compiler_params=_params(("arbitrary", "arbitrary")),
        name="mlp",
    )(x, mods, g_pre, g_post, w1, w2)


def kernel(x_prompt, x_sample, cache_k, cache_v, c, c_ctx, w_ada, b_ada, g_pre1, g_post1, g_pre2,
           g_post2, w_in, sink, w_pool, pool_scale, conv_dw, conv_b, conv_ln_g, conv_ln_b,
           w_conv_pw, w_out, w_mlp1, w_mlp2):
    cond = jnp.concatenate([c_ctx[None, :], c, jnp.zeros((N_COND - 1 - DEC_BATCH, D_MODEL), F32)], axis=0)
    mods = _adaln(cond, w_ada, b_ada).reshape(DEPTH, N_COND, 6, D_MODEL)
    x = (x_prompt.reshape(N_PROMPT, D_MODEL), x_sample.reshape(N_SAMPLE, D_MODEL))
    ck = cache_k.reshape(DEC_BATCH, DEPTH, PAST_LEN, KV_W)
    cv = cache_v.reshape(DEC_BATCH, DEPTH, PAST_LEN, KV_W)
    tables = _rope_tables()
    w_pool_b, w_pw_b = w_pool.astype(BF16), w_conv_pw.astype(BF16)
    vec = lambda a, l: a[l][None, :]
    kv_bufs = (jnp.zeros((BATCH, DEPTH, SEQ, KV_W), F32), jnp.zeros((BATCH, DEPTH, SEQ, KV_W), F32))
    w_in_b = w_in[0].astype(BF16)
    for l in range(DEPTH):
        last = l == DEPTH - 1
        z, new_k, new_v = _in_proj(x, mods[l], vec(g_pre1, l), w_in_b, l, kv_bufs)
        kv_bufs = (new_k, new_v)
        attn_ctx, w_out_b, *next_w_in = _ctx_attention(
            z, sink[l], [(w_out, l)] + ([] if last else [(w_in, l + 1)]))
        attn_lat, w1_b, w2_b = _lat_attention(z, sink[l], ck, cv, l, tables,
                                              [(w_mlp1, l), (w_mlp2, l)])
        x = _mix_out(z, attn_ctx, attn_lat, x, mods[l], vec(g_post1, l), w_out_b, w_pool_b[l],
                     vec(pool_scale, l), conv_dw[l], vec(conv_b, l), vec(conv_ln_g, l),
                     vec(conv_ln_b, l), w_pw_b[l])
        x = _mlp(x, mods[l], vec(g_pre2, l), vec(g_post2, l), w1_b, w2_b, split_out=last,
                 tm=512 if last else 1024)
        if not last:
            w_in_b = next_w_in[0]
    kv_shape = (BATCH, DEPTH, SEQ, N_KV_HEADS, HEAD_DIM)
    return (x[0].reshape(BATCH, SEQ, D_MODEL), x[1].reshape(DEC_BATCH, DEC_SEQ, D_MODEL),
            new_k.reshape(kv_shape), new_v.reshape(kv_shape))
```
